```python
import jax, jax.numpy as jnp
from jax import lax
import numpy as np

D_MODEL = 2048
BATCH = 4
SEQ = 4096
DEPTH = 2

CHUNK = 64
Q_BLOCK = 128
N_MEM = 256
D_MIX = D_MODEL
FOX_HEADS = 8
FOX_HEAD_DIM = 128
FOX_WIDTH = FOX_HEADS * FOX_HEAD_DIM
HGRN_HEADS = 8
HGRN_KEY_DIM = 128
HGRN_VAL_DIM = (D_MIX - FOX_WIDTH) // HGRN_HEADS
HGRN_KEY_WIDTH = HGRN_HEADS * HGRN_KEY_DIM
HGRN_VAL_WIDTH = HGRN_HEADS * HGRN_VAL_DIM
IN_SPLIT_SIZES = (FOX_WIDTH, FOX_WIDTH, FOX_WIDTH, FOX_HEADS,
                  HGRN_KEY_WIDTH, HGRN_KEY_WIDTH, HGRN_VAL_WIDTH, HGRN_VAL_WIDTH)
N_IN = sum(IN_SPLIT_SIZES)
IN_SPLIT_POINTS = tuple(int(v) for v in np.cumsum(IN_SPLIT_SIZES)[:-1])
XATTN_HEADS = 4
XATTN_HEAD_DIM = D_MODEL // XATTN_HEADS
D_FF = ((8 * D_MODEL // 3 + 255) // 256) * 256
CONV_WIDTH = 3
LN_EPS = 1e-5
RMS_EPS = 1e-6
MASK_VALUE = -1e30
MIN_FORGET = 1e-6
DEEPNORM_ALPHA = (2 * DEPTH) ** 0.25
DEEPNORM_BETA = (8 * DEPTH) ** -0.25

kernel_name = 'hymba_fox_hgrn2_deepnorm_convffn_encoder'


def layer_norm(x, g, b):
    xf = x.astype(jnp.float32)
    mu = jnp.mean(xf, axis=-1, keepdims=True)
    var = jnp.mean(jnp.square(xf - mu), axis=-1, keepdims=True)
    return ((xf - mu) * lax.rsqrt(var + LN_EPS) * g.astype(jnp.float32)
            + b.astype(jnp.float32)).astype(x.dtype)


def rms_norm(x, g):
    xf = x.astype(jnp.float32)
    ms = jnp.mean(jnp.square(xf), axis=-1, keepdims=True)
    return (xf * lax.rsqrt(ms + RMS_EPS) * g.astype(jnp.float32)).astype(x.dtype)


def fox_attention(q, k, v, log_f):
    B, S, H, Dh = q.shape
    nb = S // Q_BLOCK
    c = jnp.cumsum(log_f, axis=1).transpose(0, 2, 1)
    qt = q.transpose(0, 2, 1, 3) * (Dh ** -0.5)
    kt = k.transpose(0, 2, 1, 3)
    vt = v.transpose(0, 2, 1, 3)
    q_blocks = qt.reshape(B, H, nb, Q_BLOCK, Dh).transpose(2, 0, 1, 3, 4)
    c_blocks = c.reshape(B, H, nb, Q_BLOCK).transpose(2, 0, 1, 3)
    starts = jnp.arange(nb, dtype=jnp.int32) * Q_BLOCK
    key_pos = jnp.arange(S, dtype=jnp.int32)

    def block(args):
        qb, cb, start = args
        s = jnp.einsum('bhqd,bhkd->bhqk', qb, kt).astype(jnp.float32)
        q_pos = start + jnp.arange(Q_BLOCK, dtype=jnp.int32)
        mask = key_pos[None, :] <= q_pos[:, None]
        bias = jnp.where(mask, cb[..., None] - c[:, :, None, :], 0.0)
        p = jax.nn.softmax(jnp.where(mask, s + bias, MASK_VALUE), axis=-1)
        return jnp.einsum('bhqk,bhkd->bhqd', p.astype(vt.dtype), vt)

    out = lax.map(block, (q_blocks, c_blocks, starts))
    return out.transpose(1, 0, 3, 2, 4).reshape(B, S, H * Dh)


def hgrn2_chunkwise(q, k, v, log_f):
    B, S, H, dk = q.shape
    dv = v.shape[-1]
    nc = S // CHUNK

    def to_chunks(t):
        return t.reshape(B, nc, CHUNK, H, t.shape[-1]).transpose(1, 0, 3, 2, 4)

    qc = to_chunks(q.astype(jnp.float32))
    kc = to_chunks(k.astype(jnp.float32))
    vc = to_chunks(v.astype(jnp.float32))
    G = jnp.cumsum(to_chunks(log_f.astype(jnp.float32)), axis=3)
    causal = jnp.tril(jnp.ones((CHUNK, CHUNK), dtype=bool))[:, :, None]
    causal_f = causal.astype(jnp.float32)

    def step(state, inp):
        qb, kb, vb, Gb = inp
        o_inter = jnp.einsum('bhtd,bhde->bhte', qb * jnp.exp(Gb), state)
        diff = Gb[:, :, :, None, :] - Gb[:, :, None, :, :]
        decay = jnp.exp(jnp.where(causal, diff, 0.0)) * causal_f
        a = jnp.einsum('bhtsd,bhsd->bhts', qb[:, :, :, None, :] * decay, kb)
        o_intra = jnp.einsum('bhts,bhse->bhte', a, vb)
        G_last = Gb[:, :, -1, :]
        k_dec = kb * jnp.exp(G_last[:, :, None, :] - Gb)
        state = jnp.exp(G_last)[..., None] * state + jnp.einsum('bhsd,bhse->bhde', k_dec, vb)
        return state, o_inter + o_intra

    state0 = jnp.zeros((B, H, dk, dv), jnp.float32)
    _, o = lax.scan(step, state0, (qc, kc, vc, G))
    return o.transpose(1, 0, 3, 2, 4).reshape(B, S, H, dv).astype(v.dtype)


def parallel_mixer(x, w_in, fox_f_bias, lb, hgrn_norm_w, w_out):
    B, S, _ = x.shape
    proj = x @ w_in
    fq, fk, fv, ff, hq, hf, hi, hg = jnp.split(proj, IN_SPLIT_POINTS, axis=-1)
    fox_log_f = jax.nn.log_sigmoid((ff + fox_f_bias).astype(jnp.float32))
    fox_out = fox_attention(fq.reshape(B, S, FOX_HEADS, FOX_HEAD_DIM),
                            fk.reshape(B, S, FOX_HEADS, FOX_HEAD_DIM),
                            fv.reshape(B, S, FOX_HEADS, FOX_HEAD_DIM),
                            fox_log_f)
    hz = hf.astype(jnp.float32)
    f_gate = lb + (1.0 - lb) * jax.nn.sigmoid(hz)
    log_f = jnp.log(jnp.maximum(f_gate, MIN_FORGET))
    k_in = (1.0 - lb) * jax.nn.sigmoid(-hz)
    hq_h = hq.reshape(B, S, HGRN_HEADS, HGRN_KEY_DIM) * (HGRN_KEY_DIM ** -0.5)
    h_out = hgrn2_chunkwise(hq_h,
                            k_in.reshape(B, S, HGRN_HEADS, HGRN_KEY_DIM),
                            hi.reshape(B, S, HGRN_HEADS, HGRN_VAL_DIM),
                            log_f.reshape(B, S, HGRN_HEADS, HGRN_KEY_DIM))
    h_out = rms_norm(h_out, hgrn_norm_w.reshape(HGRN_HEADS, HGRN_VAL_DIM))
    h_out = h_out.reshape(B, S, HGRN_VAL_WIDTH) * jax.nn.silu(hg)
    return jnp.concatenate([fox_out, h_out.astype(fox_out.dtype)], axis=-1) @ w_out


def cross_attention(x, mem, wq, wk, wv, wo):
    B, S, _ = x.shape
    M = mem.shape[1]
    q = (x @ wq).reshape(B, S, XATTN_HEADS, XATTN_HEAD_DIM)
    k = (mem @ wk).reshape(B, M, XATTN_HEADS, XATTN_HEAD_DIM)
    v = (mem @ wv).reshape(B, M, XATTN_HEADS, XATTN_HEAD_DIM)
    s = jnp.einsum('bqhd,bkhd->bhqk', q, k).astype(jnp.float32) * (XATTN_HEAD_DIM ** -0.5)
    p = jax.nn.softmax(s, axis=-1)
    o = jnp.einsum('bhqk,bkhd->bqhd', p.astype(v.dtype), v).reshape(B, S, D_MODEL)
    return o @ wo


def conv_ffn(x, w_up, conv_w, conv_b, w_down):
    h = x @ w_up
    h = lax.conv_general_dilated(h, conv_w[:, None, :], window_strides=(1,),
                                 padding=[(CONV_WIDTH - 1, 0)],
                                 dimension_numbers=('NWC', 'WIO', 'NWC'),
                                 feature_group_count=h.shape[-1]) + conv_b
    a, b = jnp.split(h, 2, axis=-1)
    return (jax.nn.silu(a) * b) @ w_down


def setup_inputs(seed: int = 0) -> dict:
    key = jax.random.key(seed)
    ks = jax.random.split(key, 24)
    L = DEPTH

    def nrm(k, shape, scale):
        return jax.random.normal(k, shape, jnp.float32) * scale

    return {
        'x': nrm(ks[0], (BATCH, SEQ, D_MODEL), 1.0),
        'mem': nrm(ks[1], (BATCH, N_MEM, D_MODEL), 1.0),
        'w_in': nrm(ks[2], (L, D_MODEL, N_IN), D_MODEL ** -0.5),
        'fox_f_bias': nrm(ks[3], (L, FOX_HEADS), 0.1),
        'hgrn_lb_logits': nrm(ks[4], (L, HGRN_KEY_WIDTH), 0.1),
        'hgrn_norm_w': 1.0 + nrm(ks[5], (L, HGRN_VAL_WIDTH), 0.01),
        'w_out': nrm(ks[6], (L, D_MIX, D_MODEL), D_MIX ** -0.5 * DEEPNORM_BETA),
        'ln1_g': 1.0 + nrm(ks[7], (L, D_MODEL), 0.01),
        'ln1_b': nrm(ks[8], (L, D_MODEL), 0.01),
        'xq_w': nrm(ks[9], (L, D_MODEL, D_MODEL), D_MODEL ** -0.5),
        'xk_w': nrm(ks[10], (L, D_MODEL, D_MODEL), D_MODEL ** -0.5),
        'xv_w': nrm(ks[11], (L, D_MODEL, D_MODEL), D_MODEL ** -0.5 * DEEPNORM_BETA),
        'xo_w': nrm(ks[12], (L, D_MODEL, D_MODEL), D_MODEL ** -0.5 * DEEPNORM_BETA),
        'ln2_g': 1.0 + nrm(ks[13], (L, D_MODEL), 0.01),
        'ln2_b': nrm(ks[14], (L, D_MODEL), 0.01),
        'ffn_up': nrm(ks[15], (L, D_MODEL, 2 * D_FF), D_MODEL ** -0.5),
        'conv_w': nrm(ks[16], (L, CONV_WIDTH, 2 * D_FF), CONV_WIDTH ** -0.5),
        'conv_b': nrm(ks[17], (L, 2 * D_FF), 0.01),
        'ffn_down': nrm(ks[18], (L, D_FF, D_MODEL), D_FF ** -0.5 * DEEPNORM_BETA),
        'ln3_g': 1.0 + nrm(ks[19], (L, D_MODEL), 0.01),
        'ln3_b': nrm(ks[20], (L, D_MODEL), 0.01),
    }


def reference(x, mem, w_in, fox_f_bias, hgrn_lb_logits, hgrn_norm_w, w_out,
              ln1_g, ln1_b, xq_w, xk_w, xv_w, xo_w, ln2_g, ln2_b,
              ffn_up, conv_w, conv_b, ffn_down, ln3_g, ln3_b):
    lb_soft = jax.nn.softmax(hgrn_lb_logits.astype(jnp.float32), axis=0)
    lower_bounds = jnp.cumsum(lb_soft, axis=0) - lb_soft[0]
    for l in range(DEPTH):
        mix = parallel_mixer(x, w_in[l], fox_f_bias[l], lower_bounds[l], hgrn_norm_w[l], w_out[l])
        x = layer_norm(DEEPNORM_ALPHA * x + mix, ln1_g[l], ln1_b[l])
        xa = cross_attention(x, mem, xq_w[l], xk_w[l], xv_w[l], xo_w[l])
        x = layer_norm(DEEPNORM_ALPHA * x + xa, ln2_g[l], ln2_b[l])
        ff = conv_ffn(x, ffn_up[l], conv_w[l], conv_b[l], ffn_down[l])
        x = layer_norm(DEEPNORM_ALPHA * x + ff, ln3_g[l], ln3_b[l])
    return x
```

```python
import functools
import math

import jax
import jax.numpy as jnp
from jax import lax
from jax.experimental import pallas as pl
from jax.experimental.pallas import tpu as pltpu

FOX_HEADS = 8
FOX_HEAD_DIM = 128
HGRN_HEADS = 8
HGRN_DIM = 128
XATTN_HEADS = 4
CONV_WIDTH = 3
LN_EPS = 1e-5
RMS_EPS = 1e-6
MASK_VALUE = -1e30
MIN_FORGET = 1e-6
LOG2E = 1.4426950408889634

V7X_VMEM_LIMIT_BYTES = 56 * 1024 * 1024
SUBLANES = 8
HGRN_CHUNK = 128
HGRN_DIAG = 8
FGATE_ROWS = 16


def _params(*sem):
    return pltpu.CompilerParams(dimension_semantics=sem,
                                vmem_limit_bytes=V7X_VMEM_LIMIT_BYTES)


def _nt_dot(a, b):
    return lax.dot_general(a, b, (((1,), (1,)), ((), ())),
                           preferred_element_type=jnp.float32)


def _layer_norm_rows(y, g, b):
    mu = jnp.mean(y, axis=-1, keepdims=True)
    d = y - mu
    var = jnp.mean(d * d, axis=-1, keepdims=True)
    return d * lax.rsqrt(var + LN_EPS) * g + b


def _mm_kernel(a_ref, w_ref, s_ref, o_ref):
    acc = jnp.dot(a_ref[...], w_ref[...], preferred_element_type=jnp.float32)
    o_ref[...] = (acc * s_ref[...]).astype(o_ref.dtype)


def _matmul(a, w, col_scale, out_dtype, tm, tn, name):
    m, k = a.shape
    n = w.shape[1]
    tm = min(tm, m)
    return pl.pallas_call(
        _mm_kernel,
        grid=(n // tn, m // tm),
        in_specs=[pl.BlockSpec((tm, k), lambda j, i: (i, 0)),
                  pl.BlockSpec((k, tn), lambda j, i: (0, j)),
                  pl.BlockSpec((1, tn), lambda j, i: (0, j))],
        out_specs=pl.BlockSpec((tm, tn), lambda j, i: (i, j)),
        out_shape=jax.ShapeDtypeStruct((m, n), out_dtype),
        compiler_params=_params("arbitrary", "arbitrary"),
        name=name,
    )(a, w, col_scale)


def _fgate_kernel(x_ref, wt_ref, b_ref, c_ref, carry_ref, *, ts):
    @pl.when(pl.program_id(1) == 0)
    def _():
        carry_ref[...] = jnp.zeros_like(carry_ref)

    z = _nt_dot(wt_ref[...], x_ref[...]) + b_ref[...]
    lf = jnp.minimum(z, 0.0) - jnp.log(1.0 + jnp.exp(-jnp.abs(z)))
    lane = lax.broadcasted_iota(jnp.int32, lf.shape, 1)
    k = 1
    while k < ts:
        lf = lf + jnp.where(lane >= k, pltpu.roll(lf, k, axis=1), 0.0)
        k *= 2
    c = lf + carry_ref[:, 0:1]
    c_ref[0] = c
    carry_ref[...] = jnp.broadcast_to(c[:, ts - 1:ts], carry_ref.shape)


def _fox_log_decay(xb, wf_t, bias, batch, seq, ts=512):
    d = xb.shape[1]
    ts = min(ts, seq)
    nt = seq // ts
    return pl.pallas_call(
        functools.partial(_fgate_kernel, ts=ts),
        grid=(batch, nt),
        in_specs=[pl.BlockSpec((ts, d), lambda b, j: (b * nt + j, 0)),
                  pl.BlockSpec((FGATE_ROWS, d), lambda b, j: (0, 0)),
                  pl.BlockSpec((FGATE_ROWS, 1), lambda b, j: (0, 0))],
        out_specs=pl.BlockSpec((1, FGATE_ROWS, ts), lambda b, j: (b, 0, j)),
        out_shape=jax.ShapeDtypeStruct((batch, FGATE_ROWS, seq), jnp.float32),
        scratch_shapes=[pltpu.VMEM((FGATE_ROWS, 128), jnp.float32)],
        compiler_params=_params("arbitrary", "arbitrary"),
        name="fox_log_decay",
    )(xb, wf_t, bias)


def _fox_kernel(q_ref, k_ref, v_ref, c_ref, o_ref, *, tq):
    i = pl.program_id(2)
    q = q_ref[...]
    q0 = pl.multiple_of(i * tq, tq)
    c_q0 = c_ref[:, pl.ds(q0, tq)][:, 0:1]

    def scores(j):
        k0 = pl.multiple_of(j * tq, tq)
        kb = k_ref[pl.ds(k0, tq), :]
        s = _nt_dot(q, kb)
        bias = (c_q0 - c_ref[:, pl.ds(k0, tq)]) * LOG2E
        return s + bias, k0

    def update(carry, s, k0):
        m, l, acc = carry
        m_new = jnp.maximum(m, jnp.max(s, axis=-1, keepdims=True))
        p = jnp.exp2(s - m_new)
        alpha = jnp.exp2(m - m_new)
        l = alpha * l + jnp.sum(p, axis=-1, keepdims=True)
        vb = v_ref[pl.ds(k0, tq), :]
        acc = alpha * acc + jnp.dot(p.astype(vb.dtype), vb,
                                    preferred_element_type=jnp.float32)
        return m_new, l, acc

    def body(j, carry):
        s, k0 = scores(j)
        return update(carry, s, k0)

    init = (jnp.full((tq, 1), MASK_VALUE, jnp.float32),
            jnp.zeros((tq, 1), jnp.float32),
            jnp.zeros((tq, q.shape[1]), jnp.float32))
    carry = lax.fori_loop(0, i, body, init)
    s, k0 = scores(i)
    row = lax.broadcasted_iota(jnp.int32, s.shape, 0)
    col = lax.broadcasted_iota(jnp.int32, s.shape, 1)
    s = jnp.where(col <= row, s, MASK_VALUE)
    _, l, acc = update(carry, s, k0)
    o_ref[...] = (acc / l).astype(o_ref.dtype)


def _fox_attention(proj, c, batch, seq, tq=512):
    tq = min(tq, seq)
    dh = FOX_HEAD_DIM
    h = FOX_HEADS
    return pl.pallas_call(
        functools.partial(_fox_kernel, tq=tq),
        grid=(batch, h, seq // tq),
        in_specs=[pl.BlockSpec((None, tq, dh), lambda b, hh, i: (b, i, hh)),
                  pl.BlockSpec((None, seq, dh), lambda b, hh, i: (b, 0, h + hh)),
                  pl.BlockSpec((None, seq, dh), lambda b, hh, i: (b, 0, 2 * h + hh)),
                  pl.BlockSpec((None, None, 1, seq), lambda b, hh, i: (b, hh, 0, 0))],
        out_specs=pl.BlockSpec((None, tq, dh), lambda b, hh, i: (b, i, hh)),
        out_shape=jax.ShapeDtypeStruct((batch, seq, h * dh), jnp.bfloat16),
        compiler_params=_params("arbitrary", "arbitrary", "arbitrary"),
        name="fox_attention",
    )(proj, proj, proj, c)


def _group_ref(g, group, ref_row):
    c, d = g.shape
    g3 = g.reshape(c // group, group, d)
    return jnp.broadcast_to(g3[:, ref_row:ref_row + 1, :], g3.shape).reshape(c, d)


def _hgrn_kernel(q_ref, z_ref, g_ref, v_ref, lbl_ref, nw_ref, o_ref, st_ref,
                 *, layer, rows, chunk):
    @pl.when(pl.program_id(2) == 0)
    def _():
        st_ref[...] = jnp.zeros_like(st_ref)

    lg = lbl_ref[...]
    e = jnp.exp(lg - jnp.max(lg, axis=0, keepdims=True))
    soft = e / jnp.sum(e, axis=0, keepdims=True)
    cs = soft[0:1]
    for idx in range(1, layer + 1):
        cs = cs + soft[idx:idx + 1]
    lb = cs - soft[0:1]

    z = z_ref[...]
    ez = jnp.exp(-jnp.abs(z))
    r = 1.0 / (1.0 + ez)
    pos = z >= 0.0
    sig = jnp.where(pos, r, ez * r)
    nsig = jnp.where(pos, ez * r, r)
    f_gate = lb + (1.0 - lb) * sig
    logf = jnp.log(jnp.maximum(f_gate, MIN_FORGET))
    k_in = (1.0 - lb) * nsig

    row_in_chunk = lax.broadcasted_iota(jnp.int32, logf.shape, 0) & (chunk - 1)
    gcum = logf
    k = 1
    while k < chunk:
        gcum = gcum + jnp.where(row_in_chunk >= k, pltpu.roll(gcum, k, axis=0), 0.0)
        k *= 2

    rt = lax.broadcasted_iota(jnp.int32, (chunk, chunk), 0)
    cc = lax.broadcasted_iota(jnp.int32, (chunk, chunk), 1)
    rrow = lax.broadcasted_iota(jnp.int32, (chunk, HGRN_DIM), 0)
    diag_mask = ((rt // HGRN_DIAG) == (cc // HGRN_DIAG)) & (cc <= rt)

    nw = nw_ref[...]
    for c in range(rows // chunk):
        sl = slice(c * chunk, (c + 1) * chunk)
        gc = gcum[sl]
        qc = q_ref[sl, :]
        kc = k_in[sl]
        vc = v_ref[sl, :]

        a = None
        span = chunk // 2
        while span >= HGRN_DIAG:
            group = 2 * span
            gref = _group_ref(gc, group, span)
            ee = jnp.exp(-jnp.abs(gc - gref))
            upper = (rrow & (group - 1)) >= span
            eq = jnp.where(upper, ee, 0.0)
            ek = jnp.where(upper, 0.0, ee)
            p = _nt_dot((qc * eq).astype(jnp.bfloat16), (kc * ek).astype(jnp.bfloat16))
            if group < chunk:
                p = jnp.where((rt // group) == (cc // group), p, 0.0)
            a = p if a is None else a + p
            span //= 2
        dref = gc - _group_ref(gc, HGRN_DIAG, HGRN_DIAG // 2)
        p = _nt_dot((qc * jnp.exp(dref)).astype(jnp.bfloat16),
                    (kc * jnp.exp(-dref)).astype(jnp.bfloat16))
        a = a + jnp.where(diag_mask, p, 0.0)

        st = st_ref[...]
        o = jnp.dot(a.astype(jnp.bfloat16), vc, preferred_element_type=jnp.float32)
        o = o + _nt_dot((qc * jnp.exp(gc)).astype(jnp.bfloat16), st.astype(jnp.bfloat16))
        g_last = gc[chunk - 1:chunk]
        k_dec = (kc * jnp.exp(g_last - gc)).astype(jnp.bfloat16)
        upd = lax.dot_general(vc, k_dec, (((0,), (0,)), ((), ())),
                              preferred_element_type=jnp.float32)
        st_ref[...] = st * jnp.exp(g_last) + upd

        ms = jnp.mean(o * o, axis=-1, keepdims=True)
        y = o * lax.rsqrt(ms + RMS_EPS) * nw
        gate = g_ref[sl, :]
        eg = jnp.exp(-jnp.abs(gate))
        rg = 1.0 / (1.0 + eg)
        y = y * (gate * jnp.where(gate >= 0.0, rg, eg * rg))
        o_ref[sl, :] = y.astype(o_ref.dtype)


def _hgrn(hf32, proj, lb_logits, norm_w, layer, batch, seq, rows=512):
    rows = min(rows, seq)
    h = HGRN_HEADS
    d = HGRN_DIM
    depth = lb_logits.shape[0]
    return pl.pallas_call(
        functools.partial(_hgrn_kernel, layer=layer, rows=rows, chunk=HGRN_CHUNK),
        grid=(batch, h, seq // rows),
        in_specs=[pl.BlockSpec((None, rows, d), lambda b, hh, r: (b, r, hh)),
                  pl.BlockSpec((None, rows, d), lambda b, hh, r: (b, r, h + hh)),
                  pl.BlockSpec((None, rows, d), lambda b, hh, r: (b, r, 2 * h + hh)),
                  pl.BlockSpec((None, rows, d), lambda b, hh, r: (b, r, 3 * h + hh)),
                  pl.BlockSpec((depth, d), lambda b, hh, r: (0, hh)),
                  pl.BlockSpec((1, d), lambda b, hh, r: (0, hh))],
        out_specs=pl.BlockSpec((None, rows, d), lambda b, hh, r: (b, r, hh)),
        out_shape=jax.ShapeDtypeStruct((batch, seq, h * d), jnp.bfloat16),
        scratch_shapes=[pltpu.VMEM((d, d), jnp.float32)],
        compiler_params=_params("arbitrary", "arbitrary", "arbitrary"),
        name="hgrn2",
    )(hf32, hf32, hf32, proj, lb_logits, norm_w)


def _mix_out_kernel(a1_ref, a2_ref, w1_ref, w2_ref, x_ref, g_ref, b_ref, o_ref, ob_ref,
                    *, alpha):
    acc = jnp.dot(a1_ref[...], w1_ref[...], preferred_element_type=jnp.float32)
    acc = acc + jnp.dot(a2_ref[...], w2_ref[...], preferred_element_type=jnp.float32)
    o = _layer_norm_rows(alpha * x_ref[...] + acc, g_ref[...], b_ref[...])
    o_ref[...] = o
    ob_ref[...] = o.astype(ob_ref.dtype)


def _mix_out(a1, a2, w1, w2, x, g, b, alpha, tm=512):
    m, d = x.shape
    k = a1.shape[1]
    tm = min(tm, m)
    row = lambda i: (i, 0)
    fixed = lambda i: (0, 0)
    return pl.pallas_call(
        functools.partial(_mix_out_kernel, alpha=alpha),
        grid=(m // tm,),
        in_specs=[pl.BlockSpec((tm, k), row), pl.BlockSpec((tm, k), row),
                  pl.BlockSpec((k, d), fixed), pl.BlockSpec((k, d), fixed),
                  pl.BlockSpec((tm, d), row),
                  pl.BlockSpec((1, d), fixed), pl.BlockSpec((1, d), fixed)],
        out_specs=[pl.BlockSpec((tm, d), row), pl.BlockSpec((tm, d), row)],
        out_shape=[jax.ShapeDtypeStruct((m, d), jnp.float32),
                   jax.ShapeDtypeStruct((m, d), jnp.bfloat16)],
        compiler_params=_params("arbitrary"),
        name="mix_out_ln",
    )(a1, a2, w1, w2, x, g, b)


def _xattn_kernel(q_ref, k_ref, v_ref, wo_ref, x_ref, g_ref, b_ref, o_ref, ob_ref,
                  *, alpha, heads):
    d = q_ref.shape[1]
    dh = d // heads
    outs = []
    for h in range(heads):
        sl = slice(h * dh, (h + 1) * dh)
        s = _nt_dot(q_ref[:, sl], k_ref[:, sl])
        p = jnp.exp2(s - jnp.max(s, axis=-1, keepdims=True))
        p = p / jnp.sum(p, axis=-1, keepdims=True)
        outs.append(jnp.dot(p.astype(jnp.bfloat16), v_ref[:, sl],
                            preferred_element_type=jnp.float32).astype(jnp.bfloat16))
    att = jnp.concatenate(outs, axis=1)
    acc = jnp.dot(att, wo_ref[...], preferred_element_type=jnp.float32)
    o = _layer_norm_rows(alpha * x_ref[...] + acc, g_ref[...], b_ref[...])
    o_ref[...] = o
    ob_ref[...] = o.astype(ob_ref.dtype)


def _xattn(q, kv, wo, x, g, b, alpha, batch, seq, tm=512):
    m, d = x.shape
    n_mem = kv.shape[0] // batch
    tm = min(tm, seq)
    per_b = seq // tm
    row = lambda i: (i, 0)
    fixed = lambda i: (0, 0)
    return pl.pallas_call(
        functools.partial(_xattn_kernel, alpha=alpha, heads=XATTN_HEADS),
        grid=(m // tm,),
        in_specs=[pl.BlockSpec((tm, d), row),
                  pl.BlockSpec((n_mem, d), lambda i: (i // per_b, 0)),
                  pl.BlockSpec((n_mem, d), lambda i: (i // per_b, 1)),
                  pl.BlockSpec((d, d), fixed),
                  pl.BlockSpec((tm, d), row),
                  pl.BlockSpec((1, d), fixed), pl.BlockSpec((1, d), fixed)],
        out_specs=[pl.BlockSpec((tm, d), row), pl.BlockSpec((tm, d), row)],
        out_shape=[jax.ShapeDtypeStruct((m, d), jnp.float32),
                   jax.ShapeDtypeStruct((m, d), jnp.bfloat16)],
        compiler_params=_params("arbitrary"),
        name="xattn_out_ln",
    )(q, kv, kv, wo, x, g, b)


def _ffn_kernel(xb_ref, wa_ref, wb_ref, cwa_ref, cwb_ref, cba_ref, cbb_ref, wd_ref,
                x_ref, g_ref, b_ref, o_ref, ob_ref, acc_ref, ha_ref, hb_ref,
                *, alpha, tiles_per_seq):
    i = pl.program_id(0)
    j = pl.program_id(1)
    nj = pl.num_programs(1)
    tm = xb_ref.shape[0]

    @pl.when(i % tiles_per_seq == 0)
    def _():
        ha_ref[j] = jnp.zeros(ha_ref.shape[1:], ha_ref.dtype)
        hb_ref[j] = jnp.zeros(hb_ref.shape[1:], hb_ref.dtype)

    xb = xb_ref[...]

    def branch(w_ref, cw_ref, cb_ref, halo_ref):
        h = jnp.dot(xb, w_ref[...], preferred_element_type=jnp.float32)
        ext = jnp.concatenate([halo_ref[j], h], axis=0)
        halo_ref[j] = h[tm - SUBLANES:tm]
        cw = cw_ref[...]
        return (cw[0:1] * ext[SUBLANES - 2:SUBLANES - 2 + tm]
                + cw[1:2] * ext[SUBLANES - 1:SUBLANES - 1 + tm]
                + cw[2:3] * h + cb_ref[...])

    a = branch(wa_ref, cwa_ref, cba_ref, ha_ref)
    bb = branch(wb_ref, cwb_ref, cbb_ref, hb_ref)
    ea = jnp.exp(-jnp.abs(a))
    ra = 1.0 / (1.0 + ea)
    gated = (a * jnp.where(a >= 0.0, ra, ea * ra)) * bb
    contrib = jnp.dot(gated.astype(jnp.bfloat16), wd_ref[...],
                      preferred_element_type=jnp.float32)

    @pl.when(j == 0)
    def _():
        acc_ref[...] = contrib

    @pl.when(j > 0)
    def _():
        acc_ref[...] += contrib

    @pl.when(j == nj - 1)
    def _():
        o = _layer_norm_rows(alpha * x_ref[...] + acc_ref[...], g_ref[...], b_ref[...])
        o_ref[...] = o
        ob_ref[...] = o.astype(ob_ref.dtype)


def _conv_ffn(xb, w_up, conv_w, conv_b, w_down, x, g, b, alpha, seq, tm=512, tf=512):
    m, d = x.shape
    dff = w_down.shape[0]
    tm = min(tm, seq)
    nj = dff // tf
    row = lambda i, j: (i, 0)
    fixed = lambda i, j: (0, 0)
    col_a = lambda i, j: (0, j)
    col_b = lambda i, j: (0, nj + j)
    return pl.pallas_call(
        functools.partial(_ffn_kernel, alpha=alpha, tiles_per_seq=seq // tm),
        grid=(m // tm, nj),
        in_specs=[pl.BlockSpec((tm, d), row),
                  pl.BlockSpec((d, tf), col_a), pl.BlockSpec((d, tf), col_b),
                  pl.BlockSpec((CONV_WIDTH, tf), col_a), pl.BlockSpec((CONV_WIDTH, tf), col_b),
                  pl.BlockSpec((1, tf), col_a), pl.BlockSpec((1, tf), col_b),
                  pl.BlockSpec((tf, d), lambda i, j: (j, 0)),
                  pl.BlockSpec((tm, d), row),
                  pl.BlockSpec((1, d), fixed), pl.BlockSpec((1, d), fixed)],
        out_specs=[pl.BlockSpec((tm, d), row), pl.BlockSpec((tm, d), row)],
        out_shape=[jax.ShapeDtypeStruct((m, d), jnp.float32),
                   jax.ShapeDtypeStruct((m, d), jnp.bfloat16)],
        scratch_shapes=[pltpu.VMEM((tm, d), jnp.float32),
                        pltpu.VMEM((nj, SUBLANES, tf), jnp.float32),
                        pltpu.VMEM((nj, SUBLANES, tf), jnp.float32)],
        compiler_params=_params("arbitrary", "arbitrary"),
        name="conv_ffn_ln",
    )(xb, w_up, w_up, conv_w, conv_w, conv_b, conv_b, w_down, x, g, b)


def kernel(x, mem, w_in, fox_f_bias, hgrn_lb_logits, hgrn_norm_w, w_out, ln1_g, ln1_b,
           xq_w, xk_w, xv_w, xo_w, ln2_g, ln2_b, ffn_up, conv_w, conv_b, ffn_down,
           ln3_g, ln3_b):
    batch, seq, d = x.shape
    depth = w_in.shape[0]
    alpha = (2 * depth) ** 0.25
    bf = jnp.bfloat16
    fw = FOX_HEADS * FOX_HEAD_DIM
    hw = HGRN_HEADS * HGRN_DIM
    m = batch * seq
    n_mem = mem.shape[1]

    o_fq, o_fk, o_fv, o_ff = 0, fw, 2 * fw, 3 * fw
    o_hq = o_ff + FOX_HEADS
    o_hf, o_hi, o_hg = o_hq + hw, o_hq + 2 * hw, o_hq + 3 * hw

    fox_scale = jnp.concatenate(
        [jnp.full((1, fw), FOX_HEAD_DIM ** -0.5 * LOG2E, jnp.float32),
         jnp.ones((1, 3 * fw), jnp.float32)], axis=1)
    hgrn_scale = jnp.concatenate(
        [jnp.full((1, hw), HGRN_DIM ** -0.5, jnp.float32),
         jnp.ones((1, 2 * hw), jnp.float32)], axis=1)
    xq_scale = jnp.full((1, d), (d // XATTN_HEADS) ** -0.5 * LOG2E, jnp.float32)
    kv_scale = jnp.ones((1, 2 * d), jnp.float32)

    xf = x.reshape(m, d)
    xb = xf.astype(bf)
    memb = mem.reshape(batch * n_mem, d).astype(bf)

    for l in range(depth):
        wl = w_in[l]
        w_b16 = jnp.concatenate([wl[:, o_fq:o_ff], wl[:, o_hi:o_hg]], axis=1).astype(bf)
        w_f32 = jnp.concatenate([wl[:, o_hq:o_hi], wl[:, o_hg:]], axis=1).astype(bf)
        wf_t = jnp.zeros((FGATE_ROWS, d), bf).at[:FOX_HEADS].set(wl[:, o_ff:o_hq].T.astype(bf))
        f_bias = jnp.zeros((FGATE_ROWS, 1), jnp.float32).at[:FOX_HEADS, 0].set(fox_f_bias[l])

        proj = _matmul(xb, w_b16, fox_scale, bf, 1024, 1024, "in_proj_bf16")
        hf32 = _matmul(xb, w_f32, hgrn_scale, jnp.float32, 1024, 1024, "in_proj_f32")
        c = _fox_log_decay(xb, wf_t, f_bias, batch, seq)
        proj3 = proj.reshape(batch, seq, 4 * fw)
        fox_out = _fox_attention(proj3, c.reshape(batch, FGATE_ROWS, 1, seq), batch, seq)
        h_out = _hgrn(hf32.reshape(batch, seq, 3 * hw), proj3, hgrn_lb_logits,
                      hgrn_norm_w[l].reshape(1, hw), l, batch, seq)
        wo = w_out[l].astype(bf)
        xf, xb = _mix_out(fox_out.reshape(m, fw), h_out.reshape(m, hw), wo[:fw], wo[fw:],
                          xf, ln1_g[l].reshape(1, d), ln1_b[l].reshape(1, d), alpha)

        q = _matmul(xb, xq_w[l].astype(bf), xq_scale, bf, 1024, 1024, "xattn_q")
        wkv = jnp.concatenate([xk_w[l], xv_w[l]], axis=1).astype(bf)
        kv = _matmul(memb, wkv, kv_scale, bf, 1024, 1024, "xattn_kv")
        xf, xb = _xattn(q, kv, xo_w[l].astype(bf), xf, ln2_g[l].reshape(1, d),
                        ln2_b[l].reshape(1, d), alpha, batch, seq)

        xf, xb = _conv_ffn(xb, ffn_up[l].astype(bf), conv_w[l], conv_b[l].reshape(1, -1),
                           ffn_down[l].astype(bf), xf, ln3_g[l].reshape(1, d),
                           ln3_b[l].reshape(1, d), alpha, seq)
    return xf.reshape(batch, seq, d)
```

```python
import functools
import math

import jax
import jax.numpy as jnp
from jax import lax
from jax.experimental import pallas as pl
from jax.experimental.pallas import tpu as pltpu

FOX_HEADS = 8
FOX_HEAD_DIM = 128
HGRN_HEADS = 8
HGRN_DIM = 128
XATTN_HEADS = 4
CONV_WIDTH = 3
LN_EPS = 1e-5
RMS_EPS = 1e-6
MASK_VALUE = -1e30
MIN_FORGET = 1e-6
LOG2E = 1.4426950408889634

V7X_VMEM_LIMIT_BYTES = 56 * 1024 * 1024
SUBLANES = 8
BF16_SUBLANES = 16
HGRN_CHUNK = 128
HGRN_DIAG = 8
FGATE_LANES = 128
N_PIECES = 3


def _params(*sem):
    return pltpu.CompilerParams(dimension_semantics=sem,
                                vmem_limit_bytes=V7X_VMEM_LIMIT_BYTES)


def _nt_dot(a, b):
    return lax.dot_general(a, b, (((1,), (1,)), ((), ())),
                           preferred_element_type=jnp.float32)


def _layer_norm_rows(y, g, b):
    mu = jnp.mean(y, axis=-1, keepdims=True)
    d = y - mu
    var = jnp.mean(d * d, axis=-1, keepdims=True)
    return d * lax.rsqrt(var + LN_EPS) * g + b


def _mm_kernel(a_ref, w_ref, s_ref, o_ref):
    acc = jnp.dot(a_ref[...], w_ref[...], preferred_element_type=jnp.float32)
    o_ref[...] = (acc * s_ref[...]).astype(o_ref.dtype)


def _matmul(a, w, col_scale, out_dtype, tm, tn, name):
    m, k = a.shape
    n = w.shape[1]
    tm = min(tm, m)
    return pl.pallas_call(
        _mm_kernel,
        grid=(n // tn, m // tm),
        in_specs=[pl.BlockSpec((tm, k), lambda j, i: (i, 0)),
                  pl.BlockSpec((k, tn), lambda j, i: (0, j)),
                  pl.BlockSpec((1, tn), lambda j, i: (0, j))],
        out_specs=pl.BlockSpec((tm, tn), lambda j, i: (i, j)),
        out_shape=jax.ShapeDtypeStruct((m, n), out_dtype),
        compiler_params=_params("arbitrary", "arbitrary"),
        name=name,
    )(a, w, col_scale)


def _mm_t_kernel(wt_ref, a_ref, o_ref):
    o_ref[...] = _nt_dot(wt_ref[...], a_ref[...]).astype(o_ref.dtype)


def _matmul_t(wt, a, out_dtype, tm, name):
    n, k = wt.shape
    m = a.shape[0]
    tm = min(tm, m)
    return pl.pallas_call(
        _mm_t_kernel,
        grid=(m // tm,),
        in_specs=[pl.BlockSpec((n, k), lambda i: (0, 0)),
                  pl.BlockSpec((tm, k), lambda i: (i, 0))],
        out_specs=pl.BlockSpec((n, tm), lambda i: (0, i)),
        out_shape=jax.ShapeDtypeStruct((n, m), out_dtype),
        compiler_params=_params("arbitrary"),
        name=name,
    )(wt, a)


def _fgate_kernel(x_ref, wf_ref, b_ref, selk_ref, selq_ref, onek_ref, oneq_ref,
                  ka_ref, qa_ref, carry_ref, *, ts):
    @pl.when(pl.program_id(1) == 0)
    def _():
        carry_ref[...] = jnp.zeros_like(carry_ref)

    z = jnp.dot(x_ref[...], wf_ref[...], preferred_element_type=jnp.float32) + b_ref[...]
    lf = jnp.minimum(z, 0.0) - jnp.log(1.0 + jnp.exp(-jnp.abs(z)))
    row = lax.broadcasted_iota(jnp.int32, lf.shape, 0)
    k = 1
    while k < ts:
        lf = lf + jnp.where(row >= k, pltpu.roll(lf, k, axis=0), 0.0)
        k *= 2
    c = lf + carry_ref[0:1, :]
    carry_ref[...] = jnp.broadcast_to(c[ts - 1:ts, :], carry_ref.shape)
    c2 = c * LOG2E
    hi = c2.astype(jnp.bfloat16)
    r1 = c2 - hi.astype(jnp.float32)
    mid = r1.astype(jnp.bfloat16)
    lo = (r1 - mid.astype(jnp.float32)).astype(jnp.bfloat16)
    pieces = jnp.concatenate([hi, mid, lo], axis=1)
    ka = jnp.dot(pieces, selk_ref[...], preferred_element_type=jnp.float32) + onek_ref[...]
    qa = jnp.dot(pieces, selq_ref[...], preferred_element_type=jnp.float32) + oneq_ref[...]
    ka_ref[...] = ka.astype(ka_ref.dtype)
    qa_ref[...] = qa.astype(qa_ref.dtype)


def _fox_bias_operands(xb, wf, bias, batch, seq, ts=512):
    m, d = xb.shape
    ts = min(ts, seq)
    nt = seq // ts
    lanes = FGATE_LANES
    heads = FOX_HEADS
    width = heads * FOX_HEAD_DIM
    piece = jnp.arange(N_PIECES)
    head = jnp.arange(heads)
    rows = (piece[None, :] * lanes + head[:, None]).reshape(-1)
    cols_k = (head[:, None] * FOX_HEAD_DIM + piece[None, :]).reshape(-1)
    selk = jnp.zeros((N_PIECES * lanes, width), jnp.float32).at[rows, cols_k].set(-1.0)
    selq = jnp.zeros((N_PIECES * lanes, width), jnp.float32).at[rows, cols_k + N_PIECES].set(1.0)
    onek = jnp.zeros((1, width), jnp.float32).at[0, cols_k + N_PIECES].set(1.0)
    oneq = jnp.zeros((1, width), jnp.float32).at[0, cols_k].set(1.0)
    fixed = lambda b, j: (0, 0)
    row = lambda b, j: (b * nt + j, 0)
    return pl.pallas_call(
        functools.partial(_fgate_kernel, ts=ts),
        grid=(batch, nt),
        in_specs=[pl.BlockSpec((ts, d), row),
                  pl.BlockSpec((d, lanes), fixed),
                  pl.BlockSpec((1, lanes), fixed),
                  pl.BlockSpec((N_PIECES * lanes, width), fixed),
                  pl.BlockSpec((N_PIECES * lanes, width), fixed),
                  pl.BlockSpec((1, width), fixed),
                  pl.BlockSpec((1, width), fixed)],
        out_specs=[pl.BlockSpec((ts, width), row), pl.BlockSpec((ts, width), row)],
        out_shape=[jax.ShapeDtypeStruct((m, width), jnp.bfloat16),
                   jax.ShapeDtypeStruct((m, width), jnp.bfloat16)],
        scratch_shapes=[pltpu.VMEM((SUBLANES, lanes), jnp.float32)],
        compiler_params=_params("arbitrary", "arbitrary"),
        name="fox_bias_operands",
    )(xb, wf, bias, selk.astype(jnp.bfloat16), selq.astype(jnp.bfloat16), onek, oneq)


def _fox_kernel(q_ref, qa_ref, k_ref, ka_ref, vt_ref, o_ref, sc0_ref, sc1_ref, *, tq, tk, sub):
    i = pl.program_id(2)
    dh = q_ref.shape[1]
    nsub = tq // sub
    q_ext = [jnp.concatenate([q_ref[s * sub:(s + 1) * sub, :], qa_ref[s * sub:(s + 1) * sub, :]],
                             axis=1) for s in range(nsub)]

    def kv_block(k0, n):
        k_ext = jnp.concatenate([k_ref[pl.ds(k0, n), :], ka_ref[pl.ds(k0, n), :]], axis=1)
        return k_ext, vt_ref[:, pl.ds(k0, n)]

    def update(carry, s, vt):
        m, acc = carry
        m_new = jnp.maximum(m, jnp.max(s, axis=0, keepdims=True))
        p = jnp.exp2(s - m_new)
        alpha = jnp.exp2(m - m_new)
        vt_ones = jnp.concatenate([vt, jnp.ones((BF16_SUBLANES, vt.shape[1]), vt.dtype)], axis=0)
        acc = alpha * acc + jnp.dot(vt_ones, p.astype(jnp.bfloat16),
                                    preferred_element_type=jnp.float32)
        return m_new, acc

    def put_scores(j, sc_ref):
        k_ext, _ = kv_block(pl.multiple_of(j * tk, tk), tk)
        for s in range(nsub):
            sc_ref[s] = _nt_dot(k_ext, q_ext[s])

    def consume(j, sc_ref, carries):
        _, vt = kv_block(pl.multiple_of(j * tk, tk), tk)
        return tuple(update(c, sc_ref[s], vt) for s, c in enumerate(carries))

    def causal(sc, k_lo, q_lo):
        key = lax.broadcasted_iota(jnp.int32, sc.shape, 0) + k_lo
        qry = lax.broadcasted_iota(jnp.int32, sc.shape, 1) + q_lo
        return jnp.where(key <= qry, sc, MASK_VALUE)

    def body(jj, carries):
        j = 2 * jj
        put_scores(j + 1, sc1_ref)
        carries = consume(j, sc0_ref, carries)
        put_scores(j + 2, sc0_ref)
        return consume(j + 1, sc1_ref, carries)

    init = tuple((jnp.full((1, sub), MASK_VALUE, jnp.float32),
                  jnp.zeros((dh + BF16_SUBLANES, sub), jnp.float32)) for _ in range(nsub))
    put_scores(0, sc0_ref)
    carries = list(lax.fori_loop(0, i * (tq // tk // 2), body, init))

    base = i * tq
    _, vt = kv_block(pl.multiple_of(base, tk), tk)
    for s in range(nsub):
        sc = sc0_ref[s] if tk - 1 <= s * sub else causal(sc0_ref[s], 0, s * sub)
        carries[s] = update(carries[s], sc, vt)
    for kb in range(1, tq // tk):
        for s in range(nsub):
            q_lo, q_hi = s * sub, (s + 1) * sub
            k_lo = kb * tk
            k_hi = min(k_lo + tk, q_hi)
            if k_hi <= k_lo:
                continue
            k_ext, vt = kv_block(pl.multiple_of(base + k_lo, sub), k_hi - k_lo)
            sc = _nt_dot(k_ext, q_ext[s])
            if k_hi - 1 > q_lo:
                sc = causal(sc, k_lo, q_lo)
            carries[s] = update(carries[s], sc, vt)
    out = jnp.concatenate([acc[:dh] / acc[dh:dh + 1] for _, acc in carries], axis=1)
    o_ref[...] = out.T.astype(o_ref.dtype)


def _fox_attention(proj, qa, ka, vt, batch, seq, tq=1024, tk=512, sub=256):
    tq = min(tq, seq)
    assert tq % (2 * tk) == 0 and tk % sub == 0
    dh = FOX_HEAD_DIM
    h = FOX_HEADS
    score_buf = pltpu.VMEM((tq // sub, tk, sub), jnp.float32)
    q_map = lambda b, hh, i: (b, i, hh)
    k_map = lambda b, hh, i: (b, 0, hh)
    return pl.pallas_call(
        functools.partial(_fox_kernel, tq=tq, tk=tk, sub=sub),
        grid=(batch, h, seq // tq),
        in_specs=[pl.BlockSpec((None, tq, dh), q_map),
                  pl.BlockSpec((None, tq, dh), q_map),
                  pl.BlockSpec((None, seq, dh), lambda b, hh, i: (b, 0, h + hh)),
                  pl.BlockSpec((None, seq, dh), k_map),
                  pl.BlockSpec((dh, seq), lambda b, hh, i: (hh, b))],
        out_specs=pl.BlockSpec((None, tq, dh), q_map),
        out_shape=jax.ShapeDtypeStruct((batch, seq, h * dh), jnp.bfloat16),
        scratch_shapes=[score_buf, score_buf],
        compiler_params=_params("arbitrary", "arbitrary", "arbitrary"),
        name="fox_attention",
    )(proj, qa, proj, ka, vt)


def _group_ref(g, group, ref_row):
    c, d = g.shape
    g3 = g.reshape(c // group, group, d)
    return jnp.broadcast_to(g3[:, ref_row:ref_row + 1, :], g3.shape).reshape(c, d)


def _hgrn_kernel(q_ref, z_ref, g_ref, v_ref, lbl_ref, nw_ref, o_ref, st_ref,
                 *, layer, rows, chunk):
    @pl.when(pl.program_id(2) == 0)
    def _():
        st_ref[...] = jnp.zeros_like(st_ref)

    lg = lbl_ref[...]
    e = jnp.exp(lg - jnp.max(lg, axis=0, keepdims=True))
    soft = e / jnp.sum(e, axis=0, keepdims=True)
    cs = soft[0:1]
    for idx in range(1, layer + 1):
        cs = cs + soft[idx:idx + 1]
    lb = cs - soft[0:1]

    z = z_ref[...]
    ez = jnp.exp(-jnp.abs(z))
    r = 1.0 / (1.0 + ez)
    pos = z >= 0.0
    sig = jnp.where(pos, r, ez * r)
    nsig = jnp.where(pos, ez * r, r)
    f_gate = lb + (1.0 - lb) * sig
    logf = jnp.log(jnp.maximum(f_gate, MIN_FORGET))
    k_in = (1.0 - lb) * nsig

    row_in_chunk = lax.broadcasted_iota(jnp.int32, logf.shape, 0) & (chunk - 1)
    gcum = logf
    k = 1
    while k < chunk:
        gcum = gcum + jnp.where(row_in_chunk >= k, pltpu.roll(gcum, k, axis=0), 0.0)
        k *= 2

    rt = lax.broadcasted_iota(jnp.int32, (chunk, chunk), 0)
    cc = lax.broadcasted_iota(jnp.int32, (chunk, chunk), 1)
    rrow = lax.broadcasted_iota(jnp.int32, (chunk, HGRN_DIM), 0)
    diag_mask = ((rt // HGRN_DIAG) == (cc // HGRN_DIAG)) & (cc <= rt)

    nw = nw_ref[...]
    for c in range(rows // chunk):
        sl = slice(c * chunk, (c + 1) * chunk)
        gc = gcum[sl]
        qc = q_ref[sl, :]
        kc = k_in[sl]
        vc = v_ref[sl, :]

        a = None
        span = chunk // 2
        while span >= HGRN_DIAG:
            group = 2 * span
            gref = _group_ref(gc, group, span)
            ee = jnp.exp(-jnp.abs(gc - gref))
            upper = (rrow & (group - 1)) >= span
            eq = jnp.where(upper, ee, 0.0)
            ek = jnp.where(upper, 0.0, ee)
            p = _nt_dot((qc * eq).astype(jnp.bfloat16), (kc * ek).astype(jnp.bfloat16))
            if group < chunk:
                p = jnp.where((rt // group) == (cc // group), p, 0.0)
            a = p if a is None else a + p
            span //= 2
        dref = gc - _group_ref(gc, HGRN_DIAG, HGRN_DIAG // 2)
        p = _nt_dot((qc * jnp.exp(dref)).astype(jnp.bfloat16),
                    (kc * jnp.exp(-dref)).astype(jnp.bfloat16))
        a = a + jnp.where(diag_mask, p, 0.0)

        st = st_ref[...]
        o = jnp.dot(a.astype(jnp.bfloat16), vc, preferred_element_type=jnp.float32)
        o = o + _nt_dot((qc * jnp.exp(gc)).astype(jnp.bfloat16), st.astype(jnp.bfloat16))
        g_last = gc[chunk - 1:chunk]
        k_dec = (kc * jnp.exp(g_last - gc)).astype(jnp.bfloat16)
        upd = lax.dot_general(vc, k_dec, (((0,), (0,)), ((), ())),
                              preferred_element_type=jnp.float32)
        st_ref[...] = st * jnp.exp(g_last) + upd

        ms = jnp.mean(o * o, axis=-1, keepdims=True)
        y = o * lax.rsqrt(ms + RMS_EPS) * nw
        gate = g_ref[sl, :]
        eg = jnp.exp(-jnp.abs(gate))
        rg = 1.0 / (1.0 + eg)
        y = y * (gate * jnp.where(gate >= 0.0, rg, eg * rg))
        o_ref[sl, :] = y.astype(o_ref.dtype)


def _hgrn(hf32, proj, v_block0, lb_logits, norm_w, layer, batch, seq, rows=512):
    rows = min(rows, seq)
    h = HGRN_HEADS
    d = HGRN_DIM
    depth = lb_logits.shape[0]
    return pl.pallas_call(
        functools.partial(_hgrn_kernel, layer=layer, rows=rows, chunk=HGRN_CHUNK),
        grid=(batch, h, seq // rows),
        in_specs=[pl.BlockSpec((None, rows, d), lambda b, hh, r: (b, r, hh)),
                  pl.BlockSpec((None, rows, d), lambda b, hh, r: (b, r, h + hh)),
                  pl.BlockSpec((None, rows, d), lambda b, hh, r: (b, r, 2 * h + hh)),
                  pl.BlockSpec((None, rows, d), lambda b, hh, r: (b, r, v_block0 + hh)),
                  pl.BlockSpec((depth, d), lambda b, hh, r: (0, hh)),
                  pl.BlockSpec((1, d), lambda b, hh, r: (0, hh))],
        out_specs=pl.BlockSpec((None, rows, d), lambda b, hh, r: (b, r, hh)),
        out_shape=jax.ShapeDtypeStruct((batch, seq, h * d), jnp.bfloat16),
        scratch_shapes=[pltpu.VMEM((d, d), jnp.float32)],
        compiler_params=_params("arbitrary", "arbitrary", "arbitrary"),
        name="hgrn2",
    )(hf32, hf32, hf32, proj, lb_logits, norm_w)


def _mix_out_kernel(a1_ref, a2_ref, w1_ref, w2_ref, x_ref, g_ref, b_ref, o_ref, ob_ref,
                    *, alpha):
    acc = jnp.dot(a1_ref[...], w1_ref[...], preferred_element_type=jnp.float32)
    acc = acc + jnp.dot(a2_ref[...], w2_ref[...], preferred_element_type=jnp.float32)
    o = _layer_norm_rows(alpha * x_ref[...] + acc, g_ref[...], b_ref[...])
    o_ref[...] = o
    ob_ref[...] = o.astype(ob_ref.dtype)


def _mix_out(a1, a2, w1, w2, x, g, b, alpha, tm=512):
    m, d = x.shape
    k = a1.shape[1]
    tm = min(tm, m)
    row = lambda i: (i, 0)
    fixed = lambda i: (0, 0)
    return pl.pallas_call(
        functools.partial(_mix_out_kernel, alpha=alpha),
        grid=(m // tm,),
        in_specs=[pl.BlockSpec((tm, k), row), pl.BlockSpec((tm, k), row),
                  pl.BlockSpec((k, d), fixed), pl.BlockSpec((k, d), fixed),
                  pl.BlockSpec((tm, d), row),
                  pl.BlockSpec((1, d), fixed), pl.BlockSpec((1, d), fixed)],
        out_specs=[pl.BlockSpec((tm, d), row), pl.BlockSpec((tm, d), row)],
        out_shape=[jax.ShapeDtypeStruct((m, d), jnp.float32),
                   jax.ShapeDtypeStruct((m, d), jnp.bfloat16)],
        compiler_params=_params("arbitrary"),
        name="mix_out_ln",
    )(a1, a2, w1, w2, x, g, b)


def _xattn_kernel(q_ref, k_ref, v_ref, wo_ref, x_ref, g_ref, b_ref, o_ref, ob_ref,
                  *, alpha, heads):
    d = q_ref.shape[1]
    dh = d // heads
    outs = []
    for h in range(heads):
        sl = slice(h * dh, (h + 1) * dh)
        s = _nt_dot(q_ref[:, sl], k_ref[:, sl])
        p = jnp.exp2(s - jnp.max(s, axis=-1, keepdims=True))
        p = p / jnp.sum(p, axis=-1, keepdims=True)
        outs.append(jnp.dot(p.astype(jnp.bfloat16), v_ref[:, sl],
                            preferred_element_type=jnp.float32).astype(jnp.bfloat16))
    att = jnp.concatenate(outs, axis=1)
    acc = jnp.dot(att, wo_ref[...], preferred_element_type=jnp.float32)
    o = _layer_norm_rows(alpha * x_ref[...] + acc, g_ref[...], b_ref[...])
    o_ref[...] = o
    ob_ref[...] = o.astype(ob_ref.dtype)


def _xattn(q, kv, wo, x, g, b, alpha, batch, seq, tm=512):
    m, d = x.shape
    n_mem = kv.shape[0] // batch
    tm = min(tm, seq)
    per_b = seq // tm
    row = lambda i: (i, 0)
    fixed = lambda i: (0, 0)
    return pl.pallas_call(
        functools.partial(_xattn_kernel, alpha=alpha, heads=XATTN_HEADS),
        grid=(m // tm,),
        in_specs=[pl.BlockSpec((tm, d), row),
                  pl.BlockSpec((n_mem, d), lambda i: (i // per_b, 0)),
                  pl.BlockSpec((n_mem, d), lambda i: (i // per_b, 1)),
                  pl.BlockSpec((d, d), fixed),
                  pl.BlockSpec((tm, d), row),
                  pl.BlockSpec((1, d), fixed), pl.BlockSpec((1, d), fixed)],
        out_specs=[pl.BlockSpec((tm, d), row), pl.BlockSpec((tm, d), row)],
        out_shape=[jax.ShapeDtypeStruct((m, d), jnp.float32),
                   jax.ShapeDtypeStruct((m, d), jnp.bfloat16)],
        compiler_params=_params("arbitrary"),
        name="xattn_out_ln",
    )(q, kv, kv, wo, x, g, b)


def _ffn_kernel(xb_ref, wa_ref, wb_ref, cwa_ref, cwb_ref, cba_ref, cbb_ref, wd_ref,
                x_ref, g_ref, b_ref, o_ref, ob_ref, acc_ref, ha_ref, hb_ref,
                *, alpha, tiles_per_seq):
    i = pl.program_id(0)
    j = pl.program_id(1)
    nj = pl.num_programs(1)
    tm = xb_ref.shape[0]

    @pl.when(i % tiles_per_seq == 0)
    def _():
        ha_ref[j] = jnp.zeros(ha_ref.shape[1:], ha_ref.dtype)
        hb_ref[j] = jnp.zeros(hb_ref.shape[1:], hb_ref.dtype)

    xb = xb_ref[...]

    def branch(w_ref, cw_ref, cb_ref, halo_ref):
        h = jnp.dot(xb, w_ref[...], preferred_element_type=jnp.float32)
        ext = jnp.concatenate([halo_ref[j], h], axis=0)
        halo_ref[j] = h[tm - SUBLANES:tm]
        cw = cw_ref[...]
        return (cw[0:1] * ext[SUBLANES - 2:SUBLANES - 2 + tm]
                + cw[1:2] * ext[SUBLANES - 1:SUBLANES - 1 + tm]
                + cw[2:3] * h + cb_ref[...])

    a = branch(wa_ref, cwa_ref, cba_ref, ha_ref)
    bb = branch(wb_ref, cwb_ref, cbb_ref, hb_ref)
    ea = jnp.exp(-jnp.abs(a))
    ra = 1.0 / (1.0 + ea)
    gated = (a * jnp.where(a >= 0.0, ra, ea * ra)) * bb
    contrib = jnp.dot(gated.astype(jnp.bfloat16), wd_ref[...],
                      preferred_element_type=jnp.float32)

    @pl.when(j == 0)
    def _():
        acc_ref[...] = contrib

    @pl.when(j > 0)
    def _():
        acc_ref[...] += contrib

    @pl.when(j == nj - 1)
    def _():
        o = _layer_norm_rows(alpha * x_ref[...] + acc_ref[...], g_ref[...], b_ref[...])
        o_ref[...] = o
        ob_ref[...] = o.astype(ob_ref.dtype)


def _conv_ffn(xb, w_up, conv_w, conv_b, w_down, x, g, b, alpha, seq, tm=512, tf=512):
    m, d = x.shape
    dff = w_down.shape[0]
    tm = min(tm, seq)
    nj = dff // tf
    row = lambda i, j: (i, 0)
    fixed = lambda i, j: (0, 0)
    col_a = lambda i, j: (0, j)
    col_b = lambda i, j: (0, nj + j)
    return pl.pallas_call(
        functools.partial(_ffn_kernel, alpha=alpha, tiles_per_seq=seq // tm),
        grid=(m // tm, nj),
        in_specs=[pl.BlockSpec((tm, d), row),
                  pl.BlockSpec((d, tf), col_a), pl.BlockSpec((d, tf), col_b),
                  pl.BlockSpec((CONV_WIDTH, tf), col_a), pl.BlockSpec((CONV_WIDTH, tf), col_b),
                  pl.BlockSpec((1, tf), col_a), pl.BlockSpec((1, tf), col_b),
                  pl.BlockSpec((tf, d), lambda i, j: (j, 0)),
                  pl.BlockSpec((tm, d), row),
                  pl.BlockSpec((1, d), fixed), pl.BlockSpec((1, d), fixed)],
        out_specs=[pl.BlockSpec((tm, d), row), pl.BlockSpec((tm, d), row)],
        out_shape=[jax.ShapeDtypeStruct((m, d), jnp.float32),
                   jax.ShapeDtypeStruct((m, d), jnp.bfloat16)],
        scratch_shapes=[pltpu.VMEM((tm, d), jnp.float32),
                        pltpu.VMEM((nj, SUBLANES, tf), jnp.float32),
                        pltpu.VMEM((nj, SUBLANES, tf), jnp.float32)],
        compiler_params=_params("arbitrary", "arbitrary"),
        name="conv_ffn_ln",
    )(xb, w_up, w_up, conv_w, conv_w, conv_b, conv_b, w_down, x, g, b)


def kernel(x, mem, w_in, fox_f_bias, hgrn_lb_logits, hgrn_norm_w, w_out, ln1_g, ln1_b,
           xq_w, xk_w, xv_w, xo_w, ln2_g, ln2_b, ffn_up, conv_w, conv_b, ffn_down,
           ln3_g, ln3_b):
    batch, seq, d = x.shape
    depth = w_in.shape[0]
    alpha = (2 * depth) ** 0.25
    bf = jnp.bfloat16
    fw = FOX_HEADS * FOX_HEAD_DIM
    hw = HGRN_HEADS * HGRN_DIM
    m = batch * seq
    n_mem = mem.shape[1]

    o_fq, o_fk, o_fv, o_ff = 0, fw, 2 * fw, 3 * fw
    o_hq = o_ff + FOX_HEADS
    o_hf, o_hi, o_hg = o_hq + hw, o_hq + 2 * hw, o_hq + 3 * hw

    fox_scale = jnp.concatenate(
        [jnp.full((1, fw), FOX_HEAD_DIM ** -0.5 * LOG2E, jnp.float32),
         jnp.ones((1, fw + hw), jnp.float32)], axis=1)
    hgrn_scale = jnp.concatenate(
        [jnp.full((1, hw), HGRN_DIM ** -0.5, jnp.float32),
         jnp.ones((1, 2 * hw), jnp.float32)], axis=1)
    xq_scale = jnp.full((1, d), (d // XATTN_HEADS) ** -0.5 * LOG2E, jnp.float32)
    kv_scale = jnp.ones((1, 2 * d), jnp.float32)

    xf = x.reshape(m, d)
    xb = xf.astype(bf)
    memb = mem.reshape(batch * n_mem, d).astype(bf)

    for l in range(depth):
        wl = w_in[l]
        w_b16 = jnp.concatenate([wl[:, o_fq:o_fv], wl[:, o_hi:o_hg]], axis=1).astype(bf)
        w_f32 = jnp.concatenate([wl[:, o_hq:o_hi], wl[:, o_hg:]], axis=1).astype(bf)
        wv_t = wl[:, o_fv:o_ff].T.astype(bf)
        wf = jnp.zeros((d, FGATE_LANES), bf).at[:, :FOX_HEADS].set(wl[:, o_ff:o_hq].astype(bf))
        f_bias = jnp.zeros((1, FGATE_LANES), jnp.float32).at[0, :FOX_HEADS].set(fox_f_bias[l])

        proj = _matmul(xb, w_b16, fox_scale, bf, 1024, 1024, "in_proj_bf16")
        hf32 = _matmul(xb, w_f32, hgrn_scale, jnp.float32, 1024, 1024, "in_proj_f32")
        vt = _matmul_t(wv_t, xb, bf, 1024, "in_proj_vt")
        ka, qa = _fox_bias_operands(xb, wf, f_bias, batch, seq)
        proj3 = proj.reshape(batch, seq, 2 * fw + hw)
        fox_out = _fox_attention(proj3, qa.reshape(batch, seq, fw), ka.reshape(batch, seq, fw),
                                 vt, batch, seq)
        h_out = _hgrn(hf32.reshape(batch, seq, 3 * hw), proj3, 2 * fw // HGRN_DIM,
                      hgrn_lb_logits, hgrn_norm_w[l].reshape(1, hw), l, batch, seq)
        wo = w_out[l].astype(bf)
        xf, xb = _mix_out(fox_out.reshape(m, fw), h_out.reshape(m, hw), wo[:fw], wo[fw:],
                          xf, ln1_g[l].reshape(1, d), ln1_b[l].reshape(1, d), alpha)

        q = _matmul(xb, xq_w[l].astype(bf), xq_scale, bf, 1024, 1024, "xattn_q")
        wkv = jnp.concatenate([xk_w[l], xv_w[l]], axis=1).astype(bf)
        kv = _matmul(memb, wkv, kv_scale, bf, 1024, 1024, "xattn_kv")
        xf, xb = _xattn(q, kv, xo_w[l].astype(bf), xf, ln2_g[l].reshape(1, d),
                        ln2_b[l].reshape(1, d), alpha, batch, seq)

        xf, xb = _conv_ffn(xb, ffn_up[l].astype(bf), conv_w[l], conv_b[l].reshape(1, -1),
                           ffn_down[l].astype(bf), xf, ln3_g[l].reshape(1, d),
                           ln3_b[l].reshape(1, d), alpha, seq)
    return xf.reshape(batch, seq, d)
```

```python
import functools
import math

import jax
import jax.numpy as jnp
from jax import lax
from jax.experimental import pallas as pl
from jax.experimental.pallas import tpu as pltpu

FOX_HEADS = 8
FOX_HEAD_DIM = 128
HGRN_HEADS = 8
HGRN_DIM = 128
XATTN_HEADS = 4
CONV_WIDTH = 3
LN_EPS = 1e-5
RMS_EPS = 1e-6
MASK_VALUE = -1e30
MIN_FORGET = 1e-6
LOG2E = 1.4426950408889634

V7X_VMEM_LIMIT_BYTES = 56 * 1024 * 1024
SUBLANES = 8
BF16_SUBLANES = 16
HGRN_CHUNK = 128
HGRN_DIAG = 8
FGATE_LANES = 128
N_PIECES = 3


def _params(*sem):
    return pltpu.CompilerParams(dimension_semantics=sem,
                                vmem_limit_bytes=V7X_VMEM_LIMIT_BYTES)


def _nt_dot(a, b):
    return lax.dot_general(a, b, (((1,), (1,)), ((), ())),
                           preferred_element_type=jnp.float32)


def _layer_norm_rows(y, g, b):
    mu = jnp.mean(y, axis=-1, keepdims=True)
    d = y - mu
    var = jnp.mean(d * d, axis=-1, keepdims=True)
    return d * lax.rsqrt(var + LN_EPS) * g + b


def _mm_kernel(a_ref, w_ref, s_ref, o_ref, wb_ref):
    @pl.when(pl.program_id(1) == 0)
    def _():
        wb_ref[...] = w_ref[...].astype(wb_ref.dtype)

    acc = jnp.dot(a_ref[...], wb_ref[...], preferred_element_type=jnp.float32)
    o_ref[...] = (acc * s_ref[...]).astype(o_ref.dtype)


def _matmul(a, w, layer, w_blocks, col_scale, out_dtype, tm, tn, name):
    m, k = a.shape
    tm = min(tm, m)
    n_blocks = len(w_blocks)
    step = w_blocks[1] - w_blocks[0] if n_blocks > 1 else 0
    assert all(w_blocks[j] == w_blocks[0] + j * step for j in range(n_blocks))
    return pl.pallas_call(
        _mm_kernel,
        grid=(n_blocks, m // tm),
        in_specs=[pl.BlockSpec((tm, k), lambda j, i: (i, 0)),
                  pl.BlockSpec((None, k, tn), lambda j, i: (layer, 0, w_blocks[0] + j * step)),
                  pl.BlockSpec((1, tn), lambda j, i: (0, j))],
        out_specs=pl.BlockSpec((tm, tn), lambda j, i: (i, j)),
        out_shape=jax.ShapeDtypeStruct((m, n_blocks * tn), out_dtype),
        scratch_shapes=[pltpu.VMEM((k, tn), jnp.bfloat16)],
        compiler_params=_params("arbitrary", "arbitrary"),
        name=name,
    )(a, w, col_scale)


def _mm_t_kernel(w_ref, a_ref, o_ref, wt_ref):
    @pl.when(pl.program_id(0) == 0)
    def _():
        wt_ref[...] = w_ref[...].T.astype(wt_ref.dtype)

    o_ref[...] = _nt_dot(wt_ref[...], a_ref[...]).astype(o_ref.dtype)


def _matmul_t(w, layer, w_block, a, out_dtype, tm, tn, name):
    m, k = a.shape
    tm = min(tm, m)
    return pl.pallas_call(
        _mm_t_kernel,
        grid=(m // tm,),
        in_specs=[pl.BlockSpec((None, k, tn), lambda i: (layer, 0, w_block),
                               pipeline_mode=pl.Buffered(1)),
                  pl.BlockSpec((tm, k), lambda i: (i, 0))],
        out_specs=pl.BlockSpec((tn, tm), lambda i: (0, i)),
        out_shape=jax.ShapeDtypeStruct((tn, m), out_dtype),
        scratch_shapes=[pltpu.VMEM((tn, k), jnp.bfloat16)],
        compiler_params=_params("arbitrary"),
        name=name,
    )(w, a)


def _fgate_kernel(x_ref, wf_ref, b_ref, selk_ref, selq_ref, onek_ref, oneq_ref,
                  ka_ref, qa_ref, carry_ref, *, ts):
    @pl.when(pl.program_id(1) == 0)
    def _():
        carry_ref[...] = jnp.zeros_like(carry_ref)

    z = jnp.dot(x_ref[...], wf_ref[...], preferred_element_type=jnp.float32) + b_ref[...]
    lf = jnp.minimum(z, 0.0) - jnp.log(1.0 + jnp.exp(-jnp.abs(z)))
    row = lax.broadcasted_iota(jnp.int32, lf.shape, 0)
    k = 1
    while k < ts:
        lf = lf + jnp.where(row >= k, pltpu.roll(lf, k, axis=0), 0.0)
        k *= 2
    c = lf + carry_ref[0:1, :]
    carry_ref[...] = jnp.broadcast_to(c[ts - 1:ts, :], carry_ref.shape)
    c2 = c * LOG2E
    hi = c2.astype(jnp.bfloat16)
    r1 = c2 - hi.astype(jnp.float32)
    mid = r1.astype(jnp.bfloat16)
    lo = (r1 - mid.astype(jnp.float32)).astype(jnp.bfloat16)
    pieces = jnp.concatenate([hi, mid, lo], axis=1)
    ka = jnp.dot(pieces, selk_ref[...], preferred_element_type=jnp.float32) + onek_ref[...]
    qa = jnp.dot(pieces, selq_ref[...], preferred_element_type=jnp.float32) + oneq_ref[...]
    ka_ref[...] = ka.astype(ka_ref.dtype)
    qa_ref[...] = qa.astype(qa_ref.dtype)


def _fox_bias_operands(xb, wf, bias, batch, seq, ts=512):
    m, d = xb.shape
    ts = min(ts, seq)
    nt = seq // ts
    lanes = FGATE_LANES
    heads = FOX_HEADS
    width = heads * FOX_HEAD_DIM
    piece = jnp.arange(N_PIECES)
    head = jnp.arange(heads)
    rows = (piece[None, :] * lanes + head[:, None]).reshape(-1)
    cols_k = (head[:, None] * FOX_HEAD_DIM + piece[None, :]).reshape(-1)
    selk = jnp.zeros((N_PIECES * lanes, width), jnp.float32).at[rows, cols_k].set(-1.0)
    selq = jnp.zeros((N_PIECES * lanes, width), jnp.float32).at[rows, cols_k + N_PIECES].set(1.0)
    onek = jnp.zeros((1, width), jnp.float32).at[0, cols_k + N_PIECES].set(1.0)
    oneq = jnp.zeros((1, width), jnp.float32).at[0, cols_k].set(1.0)
    fixed = lambda b, j: (0, 0)
    row = lambda b, j: (b * nt + j, 0)
    return pl.pallas_call(
        functools.partial(_fgate_kernel, ts=ts),
        grid=(batch, nt),
        in_specs=[pl.BlockSpec((ts, d), row),
                  pl.BlockSpec((d, lanes), fixed),
                  pl.BlockSpec((1, lanes), fixed),
                  pl.BlockSpec((N_PIECES * lanes, width), fixed),
                  pl.BlockSpec((N_PIECES * lanes, width), fixed),
                  pl.BlockSpec((1, width), fixed),
                  pl.BlockSpec((1, width), fixed)],
        out_specs=[pl.BlockSpec((ts, width), row), pl.BlockSpec((ts, width), row)],
        out_shape=[jax.ShapeDtypeStruct((m, width), jnp.bfloat16),
                   jax.ShapeDtypeStruct((m, width), jnp.bfloat16)],
        scratch_shapes=[pltpu.VMEM((SUBLANES, lanes), jnp.float32)],
        compiler_params=_params("arbitrary", "arbitrary"),
        name="fox_bias_operands",
    )(xb, wf, bias, selk.astype(jnp.bfloat16), selq.astype(jnp.bfloat16), onek, oneq)


def _fox_kernel(q_ref, qa_ref, k_ref, ka_ref, vt_ref, o_ref, sc0_ref, sc1_ref, *, tq, tk, sub):
    i = pl.program_id(2)
    dh = q_ref.shape[1]
    nsub = tq // sub
    q_ext = [jnp.concatenate([q_ref[s * sub:(s + 1) * sub, :], qa_ref[s * sub:(s + 1) * sub, :]],
                             axis=1) for s in range(nsub)]

    def kv_block(k0, n):
        k_ext = jnp.concatenate([k_ref[pl.ds(k0, n), :], ka_ref[pl.ds(k0, n), :]], axis=1)
        return k_ext, vt_ref[:, pl.ds(k0, n)]

    def update(carry, s, vt):
        m, acc = carry
        m_new = jnp.maximum(m, jnp.max(s, axis=0, keepdims=True))
        p = jnp.exp2(s - m_new)
        alpha = jnp.exp2(m - m_new)
        vt_ones = jnp.concatenate([vt, jnp.ones((BF16_SUBLANES, vt.shape[1]), vt.dtype)], axis=0)
        acc = alpha * acc + jnp.dot(vt_ones, p.astype(jnp.bfloat16),
                                    preferred_element_type=jnp.float32)
        return m_new, acc

    def put_scores(j, sc_ref):
        k_ext, _ = kv_block(pl.multiple_of(j * tk, tk), tk)
        for s in range(nsub):
            sc_ref[s] = _nt_dot(k_ext, q_ext[s])

    def consume(j, sc_ref, carries):
        _, vt = kv_block(pl.multiple_of(j * tk, tk), tk)
        return tuple(update(c, sc_ref[s], vt) for s, c in enumerate(carries))

    def causal(sc, k_lo, q_lo):
        key = lax.broadcasted_iota(jnp.int32, sc.shape, 0) + k_lo
        qry = lax.broadcasted_iota(jnp.int32, sc.shape, 1) + q_lo
        return jnp.where(key <= qry, sc, MASK_VALUE)

    def body(jj, carries):
        j = 2 * jj
        put_scores(j + 1, sc1_ref)
        carries = consume(j, sc0_ref, carries)
        put_scores(j + 2, sc0_ref)
        return consume(j + 1, sc1_ref, carries)

    init = tuple((jnp.full((1, sub), MASK_VALUE, jnp.float32),
                  jnp.zeros((dh + BF16_SUBLANES, sub), jnp.float32)) for _ in range(nsub))
    put_scores(0, sc0_ref)
    carries = list(lax.fori_loop(0, i * (tq // tk // 2), body, init))

    base = i * tq
    _, vt = kv_block(pl.multiple_of(base, tk), tk)
    for s in range(nsub):
        sc = sc0_ref[s] if tk - 1 <= s * sub else causal(sc0_ref[s], 0, s * sub)
        carries[s] = update(carries[s], sc, vt)
    for kb in range(1, tq // tk):
        for s in range(nsub):
            q_lo, q_hi = s * sub, (s + 1) * sub
            k_lo = kb * tk
            k_hi = min(k_lo + tk, q_hi)
            if k_hi <= k_lo:
                continue
            k_ext, vt = kv_block(pl.multiple_of(base + k_lo, sub), k_hi - k_lo)
            sc = _nt_dot(k_ext, q_ext[s])
            if k_hi - 1 > q_lo:
                sc = causal(sc, k_lo, q_lo)
            carries[s] = update(carries[s], sc, vt)
    out = jnp.concatenate([acc[:dh] / acc[dh:dh + 1] for _, acc in carries], axis=1)
    o_ref[...] = out.T.astype(o_ref.dtype)


def _fox_attention(proj, qa, ka, vt, batch, seq, tq=1024, tk=512, sub=256):
    tq = min(tq, seq)
    assert tq % (2 * tk) == 0 and tk % sub == 0
    dh = FOX_HEAD_DIM
    h = FOX_HEADS
    score_buf = pltpu.VMEM((tq // sub, tk, sub), jnp.float32)
    q_map = lambda b, hh, i: (b, i, hh)
    k_map = lambda b, hh, i: (b, 0, hh)
    return pl.pallas_call(
        functools.partial(_fox_kernel, tq=tq, tk=tk, sub=sub),
        grid=(batch, h, seq // tq),
        in_specs=[pl.BlockSpec((None, tq, dh), q_map),
                  pl.BlockSpec((None, tq, dh), q_map),
                  pl.BlockSpec((None, seq, dh), lambda b, hh, i: (b, 0, h + hh)),
                  pl.BlockSpec((None, seq, dh), k_map),
                  pl.BlockSpec((dh, seq), lambda b, hh, i: (hh, b))],
        out_specs=pl.BlockSpec((None, tq, dh), q_map),
        out_shape=jax.ShapeDtypeStruct((batch, seq, h * dh), jnp.bfloat16),
        scratch_shapes=[score_buf, score_buf],
        compiler_params=_params("arbitrary", "arbitrary", "arbitrary"),
        name="fox_attention",
    )(proj, qa, proj, ka, vt)


def _group_ref(g, group, ref_row):
    c, d = g.shape
    g3 = g.reshape(c // group, group, d)
    return jnp.broadcast_to(g3[:, ref_row:ref_row + 1, :], g3.shape).reshape(c, d)


def _hgrn_kernel(q_ref, z_ref, g_ref, v_ref, lbl_ref, nw_ref, o_ref, st_ref,
                 *, layer, rows, chunk):
    @pl.when(pl.program_id(2) == 0)
    def _():
        st_ref[...] = jnp.zeros_like(st_ref)

    lg = lbl_ref[...]
    e = jnp.exp(lg - jnp.max(lg, axis=0, keepdims=True))
    soft = e / jnp.sum(e, axis=0, keepdims=True)
    cs = soft[0:1]
    for idx in range(1, layer + 1):
        cs = cs + soft[idx:idx + 1]
    lb = cs - soft[0:1]

    z = z_ref[...]
    ez = jnp.exp(-jnp.abs(z))
    r = 1.0 / (1.0 + ez)
    pos = z >= 0.0
    sig = jnp.where(pos, r, ez * r)
    nsig = jnp.where(pos, ez * r, r)
    f_gate = lb + (1.0 - lb) * sig
    logf = jnp.log(jnp.maximum(f_gate, MIN_FORGET))
    k_in = (1.0 - lb) * nsig

    row_in_chunk = lax.broadcasted_iota(jnp.int32, logf.shape, 0) & (chunk - 1)
    gcum = logf
    k = 1
    while k < chunk:
        gcum = gcum + jnp.where(row_in_chunk >= k, pltpu.roll(gcum, k, axis=0), 0.0)
        k *= 2

    rt = lax.broadcasted_iota(jnp.int32, (chunk, chunk), 0)
    cc = lax.broadcasted_iota(jnp.int32, (chunk, chunk), 1)
    rrow = lax.broadcasted_iota(jnp.int32, (chunk, HGRN_DIM), 0)
    diag_mask = ((rt // HGRN_DIAG) == (cc // HGRN_DIAG)) & (cc <= rt)

    nw = nw_ref[...]
    for c in range(rows // chunk):
        sl = slice(c * chunk, (c + 1) * chunk)
        gc = gcum[sl]
        qc = q_ref[sl, :]
        kc = k_in[sl]
        vc = v_ref[sl, :]

        a = None
        span = chunk // 2
        while span >= HGRN_DIAG:
            group = 2 * span
            gref = _group_ref(gc, group, span)
            ee = jnp.exp(-jnp.abs(gc - gref))
            upper = (rrow & (group - 1)) >= span
            eq = jnp.where(upper, ee, 0.0)
            ek = jnp.where(upper, 0.0, ee)
            p = _nt_dot((qc * eq).astype(jnp.bfloat16), (kc * ek).astype(jnp.bfloat16))
            if group < chunk:
                p = jnp.where((rt // group) == (cc // group), p, 0.0)
            a = p if a is None else a + p
            span //= 2
        dref = gc - _group_ref(gc, HGRN_DIAG, HGRN_DIAG // 2)
        p = _nt_dot((qc * jnp.exp(dref)).astype(jnp.bfloat16),
                    (kc * jnp.exp(-dref)).astype(jnp.bfloat16))
        a = a + jnp.where(diag_mask, p, 0.0)

        st = st_ref[...]
        o = jnp.dot(a.astype(jnp.bfloat16), vc, preferred_element_type=jnp.float32)
        o = o + _nt_dot((qc * jnp.exp(gc)).astype(jnp.bfloat16), st.astype(jnp.bfloat16))
        g_last = gc[chunk - 1:chunk]
        k_dec = (kc * jnp.exp(g_last - gc)).astype(jnp.bfloat16)
        upd = lax.dot_general(vc, k_dec, (((0,), (0,)), ((), ())),
                              preferred_element_type=jnp.float32)
        st_ref[...] = st * jnp.exp(g_last) + upd

        ms = jnp.mean(o * o, axis=-1, keepdims=True)
        y = o * lax.rsqrt(ms + RMS_EPS) * nw
        gate = g_ref[sl, :]
        eg = jnp.exp(-jnp.abs(gate))
        rg = 1.0 / (1.0 + eg)
        y = y * (gate * jnp.where(gate >= 0.0, rg, eg * rg))
        o_ref[sl, :] = y.astype(o_ref.dtype)


def _hgrn(hqz, hg, hv, lb_logits, norm_w, layer, batch, seq, rows=512):
    rows = min(rows, seq)
    h = HGRN_HEADS
    d = HGRN_DIM
    depth = lb_logits.shape[0]
    head = lambda b, hh, r: (b, r, hh)
    return pl.pallas_call(
        functools.partial(_hgrn_kernel, layer=layer, rows=rows, chunk=HGRN_CHUNK),
        grid=(batch, h, seq // rows),
        in_specs=[pl.BlockSpec((None, rows, d), head),
                  pl.BlockSpec((None, rows, d), lambda b, hh, r: (b, r, h + hh)),
                  pl.BlockSpec((None, rows, d), head),
                  pl.BlockSpec((None, rows, d), head),
                  pl.BlockSpec((depth, d), lambda b, hh, r: (0, hh)),
                  pl.BlockSpec((None, 1, d), lambda b, hh, r: (layer, 0, hh))],
        out_specs=pl.BlockSpec((None, rows, d), head),
        out_shape=jax.ShapeDtypeStruct((batch, seq, h * d), jnp.bfloat16),
        scratch_shapes=[pltpu.VMEM((d, d), jnp.float32)],
        compiler_params=_params("arbitrary", "arbitrary", "arbitrary"),
        name="hgrn2",
    )(hqz, hqz, hg, hv, lb_logits, norm_w)


def _mix_out_kernel(a1_ref, a2_ref, w1_ref, w2_ref, x_ref, g_ref, b_ref, o_ref, ob_ref,
                    wb_ref, *, alpha):
    @pl.when(pl.program_id(0) == 0)
    def _():
        wb_ref[0] = w1_ref[...].astype(wb_ref.dtype)
        wb_ref[1] = w2_ref[...].astype(wb_ref.dtype)

    acc = jnp.dot(a1_ref[...], wb_ref[0], preferred_element_type=jnp.float32)
    acc = acc + jnp.dot(a2_ref[...], wb_ref[1], preferred_element_type=jnp.float32)
    o = _layer_norm_rows(alpha * x_ref[...] + acc, g_ref[...], b_ref[...])
    o_ref[...] = o
    ob_ref[...] = o.astype(ob_ref.dtype)


def _resident(block_shape, index_map):
    return pl.BlockSpec(block_shape, index_map, pipeline_mode=pl.Buffered(1))


def _mix_out(a1, a2, w, x, g, b, layer, alpha, tm=512):
    m, d = x.shape
    k = a1.shape[1]
    tm = min(tm, m)
    row = lambda i: (i, 0)
    par = lambda i: (layer, 0, 0)
    return pl.pallas_call(
        functools.partial(_mix_out_kernel, alpha=alpha),
        grid=(m // tm,),
        in_specs=[pl.BlockSpec((tm, k), row), pl.BlockSpec((tm, k), row),
                  _resident((None, k, d), par), _resident((None, k, d), lambda i: (layer, 1, 0)),
                  pl.BlockSpec((tm, d), row),
                  _resident((None, 1, d), par), _resident((None, 1, d), par)],
        out_specs=[pl.BlockSpec((tm, d), row), pl.BlockSpec((tm, d), row)],
        out_shape=[jax.ShapeDtypeStruct((m, d), jnp.float32),
                   jax.ShapeDtypeStruct((m, d), jnp.bfloat16)],
        scratch_shapes=[pltpu.VMEM((2, k, d), jnp.bfloat16)],
        compiler_params=_params("arbitrary"),
        name="mix_out_ln",
    )(a1, a2, w, w, x, g, b)


def _xattn_kernel(q_ref, k_ref, v_ref, wo_ref, x_ref, g_ref, b_ref, o_ref, ob_ref,
                  wb_ref, *, alpha, heads):
    @pl.when(pl.program_id(0) == 0)
    def _():
        wb_ref[...] = wo_ref[...].astype(wb_ref.dtype)

    d = q_ref.shape[1]
    dh = d // heads
    outs = []
    for h in range(heads):
        sl = slice(h * dh, (h + 1) * dh)
        s = _nt_dot(q_ref[:, sl], k_ref[:, sl])
        p = jnp.exp2(s - jnp.max(s, axis=-1, keepdims=True))
        p = p / jnp.sum(p, axis=-1, keepdims=True)
        outs.append(jnp.dot(p.astype(jnp.bfloat16), v_ref[:, sl],
                            preferred_element_type=jnp.float32).astype(jnp.bfloat16))
    att = jnp.concatenate(outs, axis=1)
    acc = jnp.dot(att, wb_ref[...], preferred_element_type=jnp.float32)
    o = _layer_norm_rows(alpha * x_ref[...] + acc, g_ref[...], b_ref[...])
    o_ref[...] = o
    ob_ref[...] = o.astype(ob_ref.dtype)


def _xattn(q, k, v, wo, x, g, b, layer, alpha, batch, seq, tm=256):
    m, d = x.shape
    n_mem = k.shape[0] // batch
    tm = min(tm, seq)
    per_b = seq // tm
    row = lambda i: (i, 0)
    par = lambda i: (layer, 0, 0)
    mem_map = lambda i: (i // per_b, 0)
    return pl.pallas_call(
        functools.partial(_xattn_kernel, alpha=alpha, heads=XATTN_HEADS),
        grid=(m // tm,),
        in_specs=[pl.BlockSpec((tm, d), row),
                  pl.BlockSpec((n_mem, d), mem_map),
                  pl.BlockSpec((n_mem, d), mem_map),
                  _resident((None, d, d), par),
                  pl.BlockSpec((tm, d), row),
                  _resident((None, 1, d), par), _resident((None, 1, d), par)],
        out_specs=[pl.BlockSpec((tm, d), row), pl.BlockSpec((tm, d), row)],
        out_shape=[jax.ShapeDtypeStruct((m, d), jnp.float32),
                   jax.ShapeDtypeStruct((m, d), jnp.bfloat16)],
        scratch_shapes=[pltpu.VMEM((d, d), jnp.bfloat16)],
        compiler_params=_params("arbitrary"),
        name="xattn_out_ln",
    )(q, k, v, wo, x, g, b)


def _ffn_up_kernel(xb_ref, wa_ref, wb_ref, cwa_ref, cwb_ref, cba_ref, cbb_ref, o_ref,
                   wab_ref, halo_ref, *, tiles_per_seq, rs):
    i = pl.program_id(1)
    tm = xb_ref.shape[0]

    @pl.when(i == 0)
    def _():
        wab_ref[0] = wa_ref[...].astype(wab_ref.dtype)
        wab_ref[1] = wb_ref[...].astype(wab_ref.dtype)

    @pl.when(i % tiles_per_seq == 0)
    def _():
        halo_ref[...] = jnp.zeros_like(halo_ref)

    cws = (cwa_ref[...], cwb_ref[...])
    cbs = (cba_ref[...], cbb_ref[...])
    prev = [halo_ref[0], halo_ref[1]]
    for r in range(tm // rs):
        xr = xb_ref[r * rs:(r + 1) * rs, :]
        branches = []
        for t in range(2):
            h = jnp.dot(xr, wab_ref[t], preferred_element_type=jnp.float32)
            ext = jnp.concatenate([prev[t], h], axis=0)
            prev[t] = h[rs - SUBLANES:rs]
            cw = cws[t]
            branches.append(cw[0:1] * ext[SUBLANES - 2:SUBLANES - 2 + rs]
                            + cw[1:2] * ext[SUBLANES - 1:SUBLANES - 1 + rs]
                            + cw[2:3] * h + cbs[t])
        a, bb = branches
        ea = jnp.exp(-jnp.abs(a))
        ra = 1.0 / (1.0 + ea)
        gated = (a * jnp.where(a >= 0.0, ra, ea * ra)) * bb
        o_ref[r * rs:(r + 1) * rs, :] = gated.astype(o_ref.dtype)
    halo_ref[0] = prev[0]
    halo_ref[1] = prev[1]


def _ffn_up(xb, w_up, conv_w, conv_b, layer, seq, tm=1024, tf=512, rs=256):
    m, d = xb.shape
    dff = w_up.shape[2] // 2
    tm = min(tm, seq)
    nj = dff // tf
    col_a = lambda j, i: (layer, 0, j)
    col_b = lambda j, i: (layer, 0, nj + j)
    return pl.pallas_call(
        functools.partial(_ffn_up_kernel, tiles_per_seq=seq // tm, rs=rs),
        grid=(nj, m // tm),
        in_specs=[pl.BlockSpec((tm, d), lambda j, i: (i, 0)),
                  pl.BlockSpec((None, d, tf), col_a), pl.BlockSpec((None, d, tf), col_b),
                  pl.BlockSpec((None, CONV_WIDTH, tf), col_a),
                  pl.BlockSpec((None, CONV_WIDTH, tf), col_b),
                  pl.BlockSpec((None, 1, tf), col_a), pl.BlockSpec((None, 1, tf), col_b)],
        out_specs=pl.BlockSpec((tm, tf), lambda j, i: (i, j)),
        out_shape=jax.ShapeDtypeStruct((m, dff), jnp.bfloat16),
        scratch_shapes=[pltpu.VMEM((2, d, tf), jnp.bfloat16),
                        pltpu.VMEM((2, SUBLANES, tf), jnp.float32)],
        compiler_params=_params("arbitrary", "arbitrary"),
        name="ffn_up_conv_gate",
    )(xb, w_up, w_up, conv_w, conv_w, conv_b, conv_b)


def _ffn_down_kernel(a_ref, w_ref, x_ref, g_ref, b_ref, o_ref, ob_ref, *, alpha):
    acc = jnp.dot(a_ref[...], w_ref[...], preferred_element_type=jnp.float32)
    o = _layer_norm_rows(alpha * x_ref[...] + acc, g_ref[...], b_ref[...])
    o_ref[...] = o
    ob_ref[...] = o.astype(ob_ref.dtype)


def _ffn_down(a, w, x, g, b, layer, alpha, tm=256):
    m, d = x.shape
    f = a.shape[1]
    tm = min(tm, m)
    row = lambda i: (i, 0)
    par = lambda i: (layer, 0, 0)
    return pl.pallas_call(
        functools.partial(_ffn_down_kernel, alpha=alpha),
        grid=(m // tm,),
        in_specs=[pl.BlockSpec((tm, f), row),
                  _resident((f, d), lambda i: (0, 0)),
                  pl.BlockSpec((tm, d), row),
                  _resident((None, 1, d), par), _resident((None, 1, d), par)],
        out_specs=[pl.BlockSpec((tm, d), row), pl.BlockSpec((tm, d), row)],
        out_shape=[jax.ShapeDtypeStruct((m, d), jnp.float32),
                   jax.ShapeDtypeStruct((m, d), jnp.bfloat16)],
        compiler_params=_params("arbitrary"),
        name="ffn_down_ln",
    )(a, w, x, g, b)


def kernel(x, mem, w_in, fox_f_bias, hgrn_lb_logits, hgrn_norm_w, w_out, ln1_g, ln1_b,
           xq_w, xk_w, xv_w, xo_w, ln2_g, ln2_b, ffn_up, conv_w, conv_b, ffn_down,
           ln3_g, ln3_b):
    batch, seq, d = x.shape
    depth = w_in.shape[0]
    alpha = (2 * depth) ** 0.25
    bf = jnp.bfloat16
    fw = FOX_HEADS * FOX_HEAD_DIM
    hw = HGRN_HEADS * HGRN_DIM
    m = batch * seq
    n_mem = mem.shape[1]

    o_fq, o_fk, o_fv, o_ff = 0, fw, 2 * fw, 3 * fw
    o_hq = o_ff + FOX_HEADS
    o_hf, o_hi, o_hg = o_hq + hw, o_hq + 2 * hw, o_hq + 3 * hw

    tn = 1024
    ones = jnp.ones((1, tn), jnp.float32)
    qk_scale = jnp.concatenate(
        [jnp.full((1, fw), FOX_HEAD_DIM ** -0.5 * LOG2E, jnp.float32), ones], axis=1)
    hqz_scale = jnp.concatenate([jnp.full((1, hw), HGRN_DIM ** -0.5, jnp.float32), ones], axis=1)
    xq_scale = jnp.full((1, d), (d // XATTN_HEADS) ** -0.5 * LOG2E, jnp.float32)
    d_ones = jnp.ones((1, d), jnp.float32)

    xf = x.reshape(m, d)
    xb = xf.astype(bf)
    memb = mem.reshape(batch * n_mem, d).astype(bf)
    w_h = w_in[:, :, o_hq:]
    norm_w = hgrn_norm_w.reshape(depth, 1, hw)
    as_par = lambda p: p.reshape(depth, 1, -1)
    ln1 = (as_par(ln1_g), as_par(ln1_b))
    ln2 = (as_par(ln2_g), as_par(ln2_b))
    ln3 = (as_par(ln3_g), as_par(ln3_b))
    conv_b3 = as_par(conv_b)

    for l in range(depth):
        wf = jnp.zeros((d, FGATE_LANES), bf).at[:, :FOX_HEADS].set(w_in[l][:, o_ff:o_hq].astype(bf))
        f_bias = jnp.zeros((1, FGATE_LANES), jnp.float32).at[0, :FOX_HEADS].set(fox_f_bias[l])

        qk = _matmul(xb, w_in, l, (0, 1), qk_scale, bf, 1024, tn, "in_proj_qk")
        vt = _matmul_t(w_in, l, o_fv // tn, xb, bf, 1024, tn, "in_proj_vt")
        hqz = _matmul(xb, w_h, l, (0, 1), hqz_scale, jnp.float32, 1024, tn, "in_proj_hqz")
        hv = _matmul(xb, w_h, l, (2,), ones, bf, 1024, tn, "in_proj_hv")
        hg = _matmul(xb, w_h, l, (3,), ones, jnp.float32, 1024, tn, "in_proj_hg")
        ka, qa = _fox_bias_operands(xb, wf, f_bias, batch, seq)
        fox_out = _fox_attention(qk.reshape(batch, seq, 2 * fw), qa.reshape(batch, seq, fw),
                                 ka.reshape(batch, seq, fw), vt, batch, seq)
        h_out = _hgrn(hqz.reshape(batch, seq, 2 * hw), hg.reshape(batch, seq, hw),
                      hv.reshape(batch, seq, hw), hgrn_lb_logits, norm_w, l, batch, seq)
        xf, xb = _mix_out(fox_out.reshape(m, fw), h_out.reshape(m, hw), w_out, xf, *ln1, l, alpha)

        q = _matmul(xb, xq_w, l, (0, 1), xq_scale, bf, 1024, tn, "xattn_q")
        k = _matmul(memb, xk_w, l, (0, 1), d_ones, bf, 1024, tn, "xattn_k")
        v = _matmul(memb, xv_w, l, (0, 1), d_ones, bf, 1024, tn, "xattn_v")
        xf, xb = _xattn(q, k, v, xo_w, xf, *ln2, l, alpha, batch, seq)

        gated = _ffn_up(xb, ffn_up, conv_w, conv_b3, l, seq)
        xf, xb = _ffn_down(gated, ffn_down[l].astype(bf), xf, *ln3, l, alpha)
    return xf.reshape(batch, seq, d)
```

```python
import functools
import math

import jax
import jax.numpy as jnp
from jax import lax
from jax.experimental import pallas as pl
from jax.experimental.pallas import tpu as pltpu

FOX_HEADS = 8
FOX_HEAD_DIM = 128
HGRN_HEADS = 8
HGRN_DIM = 128
XATTN_HEADS = 4
CONV_WIDTH = 3
LN_EPS = 1e-5
RMS_EPS = 1e-6
MASK_VALUE = -1e30
MIN_FORGET = 1e-6
LOG2E = 1.4426950408889634

V7X_VMEM_LIMIT_BYTES = 56 * 1024 * 1024
SUBLANES = 8
LANES = 128
BF16_SUBLANES = 16
HGRN_CHUNK = 128
HGRN_DIAG = 8
FGATE_LANES = 128
N_PIECES = 3


def _params(*sem):
    return pltpu.CompilerParams(dimension_semantics=sem,
                                vmem_limit_bytes=V7X_VMEM_LIMIT_BYTES)


def _nt_dot(a, b):
    return lax.dot_general(a, b, (((1,), (1,)), ((), ())),
                           preferred_element_type=jnp.float32)


def _layer_norm_rows(y, g, b):
    mu = jnp.mean(y, axis=-1, keepdims=True)
    d = y - mu
    var = jnp.mean(d * d, axis=-1, keepdims=True)
    return d * lax.rsqrt(var + LN_EPS) * g + b


def _mm_kernel(a_ref, w_ref, wn_ref, s_ref, o_ref, wb_ref, *, lane_off):
    @pl.when(pl.program_id(1) == 0)
    def _():
        w = w_ref[...]
        if lane_off:
            tn = w.shape[1]
            w = jnp.concatenate([w, wn_ref[...]], axis=1)[:, lane_off:lane_off + tn]
        wb_ref[...] = w.astype(wb_ref.dtype)

    acc = jnp.dot(a_ref[...], wb_ref[...], preferred_element_type=jnp.float32)
    o_ref[...] = (acc * s_ref[...]).astype(o_ref.dtype)


def _matmul(a, w, layer, col0, n_blocks, col_scale, out_dtype, tm, tn, name):
    m, k = a.shape
    tm = min(tm, m)
    lane_off = col0 % LANES
    assert (col0 - lane_off) % tn == 0
    blk0 = (col0 - lane_off) // tn
    per = tn // LANES
    return pl.pallas_call(
        functools.partial(_mm_kernel, lane_off=lane_off),
        grid=(n_blocks, m // tm),
        in_specs=[pl.BlockSpec((tm, k), lambda j, i: (i, 0)),
                  pl.BlockSpec((None, k, tn), lambda j, i: (layer, 0, blk0 + j)),
                  pl.BlockSpec((None, k, LANES),
                               lambda j, i: (layer, 0, (blk0 + j + 1) * per if lane_off else 0)),
                  pl.BlockSpec((1, tn), lambda j, i: (0, j))],
        out_specs=pl.BlockSpec((tm, tn), lambda j, i: (i, j)),
        out_shape=jax.ShapeDtypeStruct((m, n_blocks * tn), out_dtype),
        scratch_shapes=[pltpu.VMEM((k, tn), jnp.bfloat16)],
        compiler_params=_params("arbitrary", "arbitrary"),
        name=name,
    )(a, w, w, col_scale)


def _cast_kernel(w_ref, o_ref):
    o_ref[...] = w_ref[...].astype(o_ref.dtype)


def _cast_layer(w, layer, rows=512):
    _, k, n = w.shape
    return pl.pallas_call(
        _cast_kernel,
        grid=(k // rows,),
        in_specs=[pl.BlockSpec((None, rows, n), lambda r: (layer, r, 0))],
        out_specs=pl.BlockSpec((rows, n), lambda r: (r, 0)),
        out_shape=jax.ShapeDtypeStruct((k, n), jnp.bfloat16),
        compiler_params=_params("arbitrary"),
        name="cast_layer_bf16",
    )(w)


def _mm_t_kernel(w_ref, a_ref, o_ref, wt_ref):
    @pl.when(pl.program_id(0) == 0)
    def _():
        wt_ref[...] = w_ref[...].T.astype(wt_ref.dtype)

    o_ref[...] = _nt_dot(wt_ref[...], a_ref[...]).astype(o_ref.dtype)


def _matmul_t(w, layer, w_block, a, out_dtype, tm, tn, name):
    m, k = a.shape
    tm = min(tm, m)
    return pl.pallas_call(
        _mm_t_kernel,
        grid=(m // tm,),
        in_specs=[pl.BlockSpec((None, k, tn), lambda i: (layer, 0, w_block),
                               pipeline_mode=pl.Buffered(1)),
                  pl.BlockSpec((tm, k), lambda i: (i, 0))],
        out_specs=pl.BlockSpec((tn, tm), lambda i: (0, i)),
        out_shape=jax.ShapeDtypeStruct((tn, m), out_dtype),
        scratch_shapes=[pltpu.VMEM((tn, k), jnp.bfloat16)],
        compiler_params=_params("arbitrary"),
        name=name,
    )(w, a)


def _fgate_kernel(x_ref, wf_ref, b_ref, selk_ref, selq_ref, onek_ref, oneq_ref,
                  ka_ref, qa_ref, carry_ref, *, ts):
    @pl.when(pl.program_id(1) == 0)
    def _():
        carry_ref[...] = jnp.zeros_like(carry_ref)

    z = jnp.dot(x_ref[...], wf_ref[...], preferred_element_type=jnp.float32) + b_ref[...]
    lf = jnp.minimum(z, 0.0) - jnp.log(1.0 + jnp.exp(-jnp.abs(z)))
    row = lax.broadcasted_iota(jnp.int32, lf.shape, 0)
    k = 1
    while k < ts:
        lf = lf + jnp.where(row >= k, pltpu.roll(lf, k, axis=0), 0.0)
        k *= 2
    c = lf + carry_ref[0:1, :]
    carry_ref[...] = jnp.broadcast_to(c[ts - 1:ts, :], carry_ref.shape)
    c2 = c * LOG2E
    hi = c2.astype(jnp.bfloat16)
    r1 = c2 - hi.astype(jnp.float32)
    mid = r1.astype(jnp.bfloat16)
    lo = (r1 - mid.astype(jnp.float32)).astype(jnp.bfloat16)
    pieces = jnp.concatenate([hi, mid, lo], axis=1)
    ka = jnp.dot(pieces, selk_ref[...], preferred_element_type=jnp.float32) + onek_ref[...]
    qa = jnp.dot(pieces, selq_ref[...], preferred_element_type=jnp.float32) + oneq_ref[...]
    ka_ref[...] = ka.astype(ka_ref.dtype)
    qa_ref[...] = qa.astype(qa_ref.dtype)


def _fox_bias_operands(xb, wf, bias, batch, seq, ts=512):
    m, d = xb.shape
    ts = min(ts, seq)
    nt = seq // ts
    lanes = FGATE_LANES
    heads = FOX_HEADS
    width = heads * FOX_HEAD_DIM
    piece = jnp.arange(N_PIECES)
    head = jnp.arange(heads)
    rows = (piece[None, :] * lanes + head[:, None]).reshape(-1)
    cols_k = (head[:, None] * FOX_HEAD_DIM + piece[None, :]).reshape(-1)
    selk = jnp.zeros((N_PIECES * lanes, width), jnp.float32).at[rows, cols_k].set(-1.0)
    selq = jnp.zeros((N_PIECES * lanes, width), jnp.float32).at[rows, cols_k + N_PIECES].set(1.0)
    onek = jnp.zeros((1, width), jnp.float32).at[0, cols_k + N_PIECES].set(1.0)
    oneq = jnp.zeros((1, width), jnp.float32).at[0, cols_k].set(1.0)
    fixed = lambda b, j: (0, 0)
    row = lambda b, j: (b * nt + j, 0)
    return pl.pallas_call(
        functools.partial(_fgate_kernel, ts=ts),
        grid=(batch, nt),
        in_specs=[pl.BlockSpec((ts, d), row),
                  pl.BlockSpec((d, lanes), fixed),
                  pl.BlockSpec((1, lanes), fixed),
                  pl.BlockSpec((N_PIECES * lanes, width), fixed),
                  pl.BlockSpec((N_PIECES * lanes, width), fixed),
                  pl.BlockSpec((1, width), fixed),
                  pl.BlockSpec((1, width), fixed)],
        out_specs=[pl.BlockSpec((ts, width), row), pl.BlockSpec((ts, width), row)],
        out_shape=[jax.ShapeDtypeStruct((m, width), jnp.bfloat16),
                   jax.ShapeDtypeStruct((m, width), jnp.bfloat16)],
        scratch_shapes=[pltpu.VMEM((SUBLANES, lanes), jnp.float32)],
        compiler_params=_params("arbitrary", "arbitrary"),
        name="fox_bias_operands",
    )(xb, wf, bias, selk.astype(jnp.bfloat16), selq.astype(jnp.bfloat16), onek, oneq)


def _fox_kernel(q_ref, qa_ref, k_ref, ka_ref, vt_ref, o_ref, sc0_ref, sc1_ref, *, tq, tk, sub):
    i = pl.program_id(2)
    dh = q_ref.shape[1]
    nsub = tq // sub
    q_ext = [jnp.concatenate([q_ref[s * sub:(s + 1) * sub, :], qa_ref[s * sub:(s + 1) * sub, :]],
                             axis=1) for s in range(nsub)]

    def kv_block(k0, n):
        k_ext = jnp.concatenate([k_ref[pl.ds(k0, n), :], ka_ref[pl.ds(k0, n), :]], axis=1)
        return k_ext, vt_ref[:, pl.ds(k0, n)]

    def update(carry, s, vt):
        m, acc = carry
        m_new = jnp.maximum(m, jnp.max(s, axis=0, keepdims=True))
        p = jnp.exp2(s - m_new)
        alpha = jnp.exp2(m - m_new)
        vt_ones = jnp.concatenate([vt, jnp.ones((BF16_SUBLANES, vt.shape[1]), vt.dtype)], axis=0)
        acc = alpha * acc + jnp.dot(vt_ones, p.astype(jnp.bfloat16),
                                    preferred_element_type=jnp.float32)
        return m_new, acc

    def put_scores(j, sc_ref):
        k_ext, _ = kv_block(pl.multiple_of(j * tk, tk), tk)
        for s in range(nsub):
            sc_ref[s] = _nt_dot(k_ext, q_ext[s])

    def consume(j, sc_ref, carries):
        _, vt = kv_block(pl.multiple_of(j * tk, tk), tk)
        return tuple(update(c, sc_ref[s], vt) for s, c in enumerate(carries))

    def causal(sc, k_lo, q_lo):
        key = lax.broadcasted_iota(jnp.int32, sc.shape, 0) + k_lo
        qry = lax.broadcasted_iota(jnp.int32, sc.shape, 1) + q_lo
        return jnp.where(key <= qry, sc, MASK_VALUE)

    def body(jj, carries):
        j = 2 * jj
        put_scores(j + 1, sc1_ref)
        carries = consume(j, sc0_ref, carries)
        put_scores(j + 2, sc0_ref)
        return consume(j + 1, sc1_ref, carries)

    init = tuple((jnp.full((1, sub), MASK_VALUE, jnp.float32),
                  jnp.zeros((dh + BF16_SUBLANES, sub), jnp.float32)) for _ in range(nsub))
    put_scores(0, sc0_ref)
    carries = list(lax.fori_loop(0, i * (tq // tk // 2), body, init))

    base = i * tq
    _, vt = kv_block(pl.multiple_of(base, tk), tk)
    for s in range(nsub):
        sc = sc0_ref[s] if tk - 1 <= s * sub else causal(sc0_ref[s], 0, s * sub)
        carries[s] = update(carries[s], sc, vt)
    for kb in range(1, tq // tk):
        for s in range(nsub):
            q_lo, q_hi = s * sub, (s + 1) * sub
            k_lo = kb * tk
            k_hi = min(k_lo + tk, q_hi)
            if k_hi <= k_lo:
                continue
            k_ext, vt = kv_block(pl.multiple_of(base + k_lo, sub), k_hi - k_lo)
            sc = _nt_dot(k_ext, q_ext[s])
            if k_hi - 1 > q_lo:
                sc = causal(sc, k_lo, q_lo)
            carries[s] = update(carries[s], sc, vt)
    out = jnp.concatenate([acc[:dh] / acc[dh:dh + 1] for _, acc in carries], axis=1)
    o_ref[...] = out.T.astype(o_ref.dtype)


def _fox_attention(proj, qa, ka, vt, batch, seq, tq=1024, tk=512, sub=256):
    tq = min(tq, seq)
    assert tq % (2 * tk) == 0 and tk % sub == 0
    dh = FOX_HEAD_DIM
    h = FOX_HEADS
    score_buf = pltpu.VMEM((tq // sub, tk, sub), jnp.float32)
    q_map = lambda b, hh, i: (b, i, hh)
    k_map = lambda b, hh, i: (b, 0, hh)
    return pl.pallas_call(
        functools.partial(_fox_kernel, tq=tq, tk=tk, sub=sub),
        grid=(batch, h, seq // tq),
        in_specs=[pl.BlockSpec((None, tq, dh), q_map),
                  pl.BlockSpec((None, tq, dh), q_map),
                  pl.BlockSpec((None, seq, dh), lambda b, hh, i: (b, 0, h + hh)),
                  pl.BlockSpec((None, seq, dh), k_map),
                  pl.BlockSpec((dh, seq), lambda b, hh, i: (hh, b))],
        out_specs=pl.BlockSpec((None, tq, dh), q_map),
        out_shape=jax.ShapeDtypeStruct((batch, seq, h * dh), jnp.bfloat16),
        scratch_shapes=[score_buf, score_buf],
        compiler_params=_params("arbitrary", "arbitrary", "arbitrary"),
        name="fox_attention",
    )(proj, qa, proj, ka, vt)


def _group_ref(g, group, ref_row):
    c, d = g.shape
    g3 = g.reshape(c // group, group, d)
    return jnp.broadcast_to(g3[:, ref_row:ref_row + 1, :], g3.shape).reshape(c, d)


def _hgrn_kernel(q_ref, z_ref, g_ref, v_ref, lbl_ref, nw_ref, o_ref, st_ref,
                 *, layer, rows, chunk):
    @pl.when(pl.program_id(2) == 0)
    def _():
        st_ref[...] = jnp.zeros_like(st_ref)

    lg = lbl_ref[...]
    e = jnp.exp(lg - jnp.max(lg, axis=0, keepdims=True))
    soft = e / jnp.sum(e, axis=0, keepdims=True)
    cs = soft[0:1]
    for idx in range(1, layer + 1):
        cs = cs + soft[idx:idx + 1]
    lb = cs - soft[0:1]

    z = z_ref[...]
    ez = jnp.exp(-jnp.abs(z))
    r = 1.0 / (1.0 + ez)
    pos = z >= 0.0
    sig = jnp.where(pos, r, ez * r)
    nsig = jnp.where(pos, ez * r, r)
    f_gate = lb + (1.0 - lb) * sig
    logf = jnp.log(jnp.maximum(f_gate, MIN_FORGET))
    k_in = (1.0 - lb) * nsig

    rt = lax.broadcasted_iota(jnp.int32, (chunk, chunk), 0)
    cc = lax.broadcasted_iota(jnp.int32, (chunk, chunk), 1)
    diag_mask = ((rt // HGRN_DIAG) == (cc // HGRN_DIAG)) & (cc <= rt)
    tri_r = lax.broadcasted_iota(jnp.int32, (chunk, N_PIECES * chunk), 0)
    tri_c = lax.broadcasted_iota(jnp.int32, (chunk, N_PIECES * chunk), 1) & (chunk - 1)
    tri = jnp.where(tri_c <= tri_r, 1.0, 0.0).astype(jnp.bfloat16)
    zeros = {}
    span = chunk // 2
    while span >= HGRN_DIAG:
        zeros[span] = jnp.zeros((chunk // (2 * span), span, HGRN_DIM), jnp.float32)
        span //= 2

    nw = nw_ref[...]
    n_chunks = rows // chunk
    slices = [slice(c * chunk, (c + 1) * chunk) for c in range(n_chunks)]

    hi = logf.astype(jnp.bfloat16)
    r1 = logf - hi.astype(jnp.float32)
    mid = r1.astype(jnp.bfloat16)
    lo = (r1 - mid.astype(jnp.float32)).astype(jnp.bfloat16)
    gcs = [jnp.dot(tri, jnp.concatenate([hi[sl], mid[sl], lo[sl]], axis=0),
                   preferred_element_type=jnp.float32) for sl in slices]

    intra = []
    for c, sl in enumerate(slices):
        gc = gcs[c]
        qc = q_ref[sl, :]
        kc = k_in[sl]

        a = None
        span = chunk // 2
        while span >= HGRN_DIAG:
            group = 2 * span
            ng = chunk // group
            g3 = gc.reshape(ng, group, HGRN_DIM)
            gref = g3[:, span:span + 1, :]
            q_up = qc.reshape(ng, group, HGRN_DIM)[:, span:, :] * jnp.exp(g3[:, span:, :] - gref)
            k_lo = kc.reshape(ng, group, HGRN_DIM)[:, :span, :] * jnp.exp(gref - g3[:, :span, :])
            ql = jnp.concatenate([zeros[span], q_up], axis=1).reshape(chunk, HGRN_DIM)
            kl = jnp.concatenate([k_lo, zeros[span]], axis=1).reshape(chunk, HGRN_DIM)
            p = _nt_dot(ql.astype(jnp.bfloat16), kl.astype(jnp.bfloat16))
            a = p if a is None else jnp.where((rt // group) == (cc // group), p, a)
            span //= 2
        dref = gc - _group_ref(gc, HGRN_DIAG, HGRN_DIAG // 2)
        p = _nt_dot((qc * jnp.exp(dref)).astype(jnp.bfloat16),
                    (kc * jnp.exp(-dref)).astype(jnp.bfloat16))
        intra.append(jnp.where(diag_mask, p, a).astype(jnp.bfloat16))

    o_intra, upds, qgs, decs = [], [], [], []
    for c, sl in enumerate(slices):
        gc = gcs[c]
        vc = v_ref[sl, :]
        g_last = gc[chunk - 1:chunk]
        k_dec = (k_in[sl] * jnp.exp(g_last - gc)).astype(jnp.bfloat16)
        o_intra.append(jnp.dot(intra[c], vc, preferred_element_type=jnp.float32))
        upds.append(lax.dot_general(vc, k_dec, (((0,), (0,)), ((), ())),
                                    preferred_element_type=jnp.float32))
        qgs.append((q_ref[sl, :] * jnp.exp(gc)).astype(jnp.bfloat16))
        decs.append(jnp.exp(g_last))

    st = st_ref[...]
    for c, sl in enumerate(slices):
        o = o_intra[c] + _nt_dot(qgs[c], st.astype(jnp.bfloat16))
        st = st * decs[c] + upds[c]

        ms = jnp.mean(o * o, axis=-1, keepdims=True)
        y = o * lax.rsqrt(ms + RMS_EPS) * nw
        gate = g_ref[sl, :]
        eg = jnp.exp(-jnp.abs(gate))
        rg = 1.0 / (1.0 + eg)
        y = y * (gate * jnp.where(gate >= 0.0, rg, eg * rg))
        o_ref[sl, :] = y.astype(o_ref.dtype)
    st_ref[...] = st


def _hgrn(hqz, hg, hv, lb_logits, norm_w, layer, batch, seq, rows=512):
    rows = min(rows, seq)
    h = HGRN_HEADS
    d = HGRN_DIM
    depth = lb_logits.shape[0]
    head = lambda b, hh, r: (b, r, hh)
    return pl.pallas_call(
        functools.partial(_hgrn_kernel, layer=layer, rows=rows, chunk=HGRN_CHUNK),
        grid=(batch, h, seq // rows),
        in_specs=[pl.BlockSpec((None, rows, d), head),
                  pl.BlockSpec((None, rows, d), lambda b, hh, r: (b, r, h + hh)),
                  pl.BlockSpec((None, rows, d), head),
                  pl.BlockSpec((None, rows, d), head),
                  pl.BlockSpec((depth, d), lambda b, hh, r: (0, hh)),
                  pl.BlockSpec((None, 1, d), lambda b, hh, r: (layer, 0, hh))],
        out_specs=pl.BlockSpec((None, rows, d), head),
        out_shape=jax.ShapeDtypeStruct((batch, seq, h * d), jnp.bfloat16),
        scratch_shapes=[pltpu.VMEM((d, d), jnp.float32)],
        compiler_params=_params("arbitrary", "arbitrary", "arbitrary"),
        name="hgrn2",
    )(hqz, hqz, hg, hv, lb_logits, norm_w)


def _mix_out_kernel(a1_ref, a2_ref, w1_ref, w2_ref, x_ref, g_ref, b_ref, o_ref, ob_ref,
                    wb_ref, *, alpha):
    @pl.when(pl.program_id(0) == 0)
    def _():
        wb_ref[0] = w1_ref[...].astype(wb_ref.dtype)
        wb_ref[1] = w2_ref[...].astype(wb_ref.dtype)

    acc = jnp.dot(a1_ref[...], wb_ref[0], preferred_element_type=jnp.float32)
    acc = acc + jnp.dot(a2_ref[...], wb_ref[1], preferred_element_type=jnp.float32)
    o = _layer_norm_rows(alpha * x_ref[...] + acc, g_ref[...], b_ref[...])
    o_ref[...] = o
    ob_ref[...] = o.astype(ob_ref.dtype)


def _resident(block_shape, index_map):
    return pl.BlockSpec(block_shape, index_map, pipeline_mode=pl.Buffered(1))


def _mix_out(a1, a2, w, x, g, b, layer, alpha, tm=512):
    m, d = x.shape
    k = a1.shape[1]
    tm = min(tm, m)
    row = lambda i: (i, 0)
    par = lambda i: (layer, 0, 0)
    return pl.pallas_call(
        functools.partial(_mix_out_kernel, alpha=alpha),
        grid=(m // tm,),
        in_specs=[pl.BlockSpec((tm, k), row), pl.BlockSpec((tm, k), row),
                  _resident((None, k, d), par), _resident((None, k, d), lambda i: (layer, 1, 0)),
                  pl.BlockSpec((tm, d), row),
                  _resident((None, 1, d), par), _resident((None, 1, d), par)],
        out_specs=[pl.BlockSpec((tm, d), row), pl.BlockSpec((tm, d), row)],
        out_shape=[jax.ShapeDtypeStruct((m, d), jnp.float32),
                   jax.ShapeDtypeStruct((m, d), jnp.bfloat16)],
        scratch_shapes=[pltpu.VMEM((2, k, d), jnp.bfloat16)],
        compiler_params=_params("arbitrary"),
        name="mix_out_ln",
    )(a1, a2, w, w, x, g, b)


def _xattn_kernel(q_ref, k_ref, v_ref, wo_ref, x_ref, g_ref, b_ref, o_ref, ob_ref,
                  wb_ref, *, alpha, heads):
    @pl.when(pl.program_id(0) == 0)
    def _():
        wb_ref[...] = wo_ref[...].astype(wb_ref.dtype)

    d = q_ref.shape[1]
    dh = d // heads
    outs = []
    scores = [_nt_dot(q_ref[:, h * dh:(h + 1) * dh], k_ref[:, h * dh:(h + 1) * dh])
              for h in range(heads)]
    for h in range(heads):
        sl = slice(h * dh, (h + 1) * dh)
        s = scores[h]
        p = jnp.exp2(s - jnp.max(s, axis=-1, keepdims=True))
        p = p / jnp.sum(p, axis=-1, keepdims=True)
        outs.append(jnp.dot(p.astype(jnp.bfloat16), v_ref[:, sl],
                            preferred_element_type=jnp.float32).astype(jnp.bfloat16))
    att = jnp.concatenate(outs, axis=1)
    acc = jnp.dot(att, wb_ref[...], preferred_element_type=jnp.float32)
    o = _layer_norm_rows(alpha * x_ref[...] + acc, g_ref[...], b_ref[...])
    o_ref[...] = o
    ob_ref[...] = o.astype(ob_ref.dtype)


def _xattn(q, k, v, wo, x, g, b, layer, alpha, batch, seq, tm=256):
    m, d = x.shape
    n_mem = k.shape[0] // batch
    tm = min(tm, seq)
    per_b = seq // tm
    row = lambda i: (i, 0)
    par = lambda i: (layer, 0, 0)
    mem_map = lambda i: (i // per_b, 0)
    return pl.pallas_call(
        functools.partial(_xattn_kernel, alpha=alpha, heads=XATTN_HEADS),
        grid=(m // tm,),
        in_specs=[pl.BlockSpec((tm, d), row),
                  pl.BlockSpec((n_mem, d), mem_map),
                  pl.BlockSpec((n_mem, d), mem_map),
                  _resident((None, d, d), par),
                  pl.BlockSpec((tm, d), row),
                  _resident((None, 1, d), par), _resident((None, 1, d), par)],
        out_specs=[pl.BlockSpec((tm, d), row), pl.BlockSpec((tm, d), row)],
        out_shape=[jax.ShapeDtypeStruct((m, d), jnp.float32),
                   jax.ShapeDtypeStruct((m, d), jnp.bfloat16)],
        scratch_shapes=[pltpu.VMEM((d, d), jnp.bfloat16)],
        compiler_params=_params("arbitrary"),
        name="xattn_out_ln",
    )(q, k, v, wo, x, g, b)


def _ffn_up_kernel(xb_ref, wa_ref, wb_ref, cwa_ref, cwb_ref, cba_ref, cbb_ref, o_ref,
                   wab_ref, halo_ref, *, tiles_per_seq, rs):
    i = pl.program_id(1)
    tm = xb_ref.shape[0]

    @pl.when(i == 0)
    def _():
        wab_ref[0] = wa_ref[...].astype(wab_ref.dtype)
        wab_ref[1] = wb_ref[...].astype(wab_ref.dtype)

    @pl.when(i % tiles_per_seq == 0)
    def _():
        halo_ref[...] = jnp.zeros_like(halo_ref)

    cws = (cwa_ref[...], cwb_ref[...])
    cbs = (cba_ref[...], cbb_ref[...])
    prev = [halo_ref[0], halo_ref[1]]
    for r in range(tm // rs):
        xr = xb_ref[r * rs:(r + 1) * rs, :]
        branches = []
        for t in range(2):
            h = jnp.dot(xr, wab_ref[t], preferred_element_type=jnp.float32)
            seam = jnp.concatenate([prev[t], h[:SUBLANES]], axis=0)
            h1 = jnp.concatenate([seam[SUBLANES - 1:2 * SUBLANES - 1],
                                  pltpu.roll(h, 1, axis=0)[SUBLANES:]], axis=0)
            h2 = jnp.concatenate([seam[SUBLANES - 2:2 * SUBLANES - 2],
                                  pltpu.roll(h, 2, axis=0)[SUBLANES:]], axis=0)
            prev[t] = h[rs - SUBLANES:rs]
            cw = cws[t]
            branches.append(cw[0:1] * h2 + cw[1:2] * h1 + cw[2:3] * h + cbs[t])
        a, bb = branches
        ea = jnp.exp(-jnp.abs(a))
        ra = 1.0 / (1.0 + ea)
        gated = (a * jnp.where(a >= 0.0, ra, ea * ra)) * bb
        o_ref[r * rs:(r + 1) * rs, :] = gated.astype(o_ref.dtype)
    halo_ref[0] = prev[0]
    halo_ref[1] = prev[1]


def _ffn_up(xb, w_up, conv_w, conv_b, layer, seq, tm=1024, tf=512, rs=256):
    m, d = xb.shape
    dff = w_up.shape[2] // 2
    tm = min(tm, seq)
    nj = dff // tf
    col_a = lambda j, i: (layer, 0, j)
    col_b = lambda j, i: (layer, 0, nj + j)
    return pl.pallas_call(
        functools.partial(_ffn_up_kernel, tiles_per_seq=seq // tm, rs=rs),
        grid=(nj, m // tm),
        in_specs=[pl.BlockSpec((tm, d), lambda j, i: (i, 0)),
                  pl.BlockSpec((None, d, tf), col_a), pl.BlockSpec((None, d, tf), col_b),
                  pl.BlockSpec((None, CONV_WIDTH, tf), col_a),
                  pl.BlockSpec((None, CONV_WIDTH, tf), col_b),
                  pl.BlockSpec((None, 1, tf), col_a), pl.BlockSpec((None, 1, tf), col_b)],
        out_specs=pl.BlockSpec((tm, tf), lambda j, i: (i, j)),
        out_shape=jax.ShapeDtypeStruct((m, dff), jnp.bfloat16),
        scratch_shapes=[pltpu.VMEM((2, d, tf), jnp.bfloat16),
                        pltpu.VMEM((2, SUBLANES, tf), jnp.float32)],
        compiler_params=_params("arbitrary", "arbitrary"),
        name="ffn_up_conv_gate",
    )(xb, w_up, w_up, conv_w, conv_w, conv_b, conv_b)


def _ffn_down_kernel(a_ref, w_ref, x_ref, g_ref, b_ref, o_ref, ob_ref, *, alpha):
    acc = jnp.dot(a_ref[...], w_ref[...], preferred_element_type=jnp.float32)
    o = _layer_norm_rows(alpha * x_ref[...] + acc, g_ref[...], b_ref[...])
    o_ref[...] = o
    ob_ref[...] = o.astype(ob_ref.dtype)


def _ffn_down(a, w, x, g, b, layer, alpha, tm=256):
    m, d = x.shape
    f = a.shape[1]
    tm = min(tm, m)
    row = lambda i: (i, 0)
    par = lambda i: (layer, 0, 0)
    return pl.pallas_call(
        functools.partial(_ffn_down_kernel, alpha=alpha),
        grid=(m // tm,),
        in_specs=[pl.BlockSpec((tm, f), row),
                  _resident((f, d), lambda i: (0, 0)),
                  pl.BlockSpec((tm, d), row),
                  _resident((None, 1, d), par), _resident((None, 1, d), par)],
        out_specs=[pl.BlockSpec((tm, d), row), pl.BlockSpec((tm, d), row)],
        out_shape=[jax.ShapeDtypeStruct((m, d), jnp.float32),
                   jax.ShapeDtypeStruct((m, d), jnp.bfloat16)],
        compiler_params=_params("arbitrary"),
        name="ffn_down_ln",
    )(a, w, x, g, b)


def kernel(x, mem, w_in, fox_f_bias, hgrn_lb_logits, hgrn_norm_w, w_out, ln1_g, ln1_b,
           xq_w, xk_w, xv_w, xo_w, ln2_g, ln2_b, ffn_up, conv_w, conv_b, ffn_down,
           ln3_g, ln3_b):
    batch, seq, d = x.shape
    depth = w_in.shape[0]
    alpha = (2 * depth) ** 0.25
    bf = jnp.bfloat16
    fw = FOX_HEADS * FOX_HEAD_DIM
    hw = HGRN_HEADS * HGRN_DIM
    m = batch * seq
    n_mem = mem.shape[1]

    o_fq, o_fk, o_fv, o_ff = 0, fw, 2 * fw, 3 * fw
    o_hq = o_ff + FOX_HEADS
    o_hf, o_hi, o_hg = o_hq + hw, o_hq + 2 * hw, o_hq + 3 * hw

    tn = 1024
    ones = jnp.ones((1, tn), jnp.float32)
    qk_scale = jnp.concatenate(
        [jnp.full((1, fw), FOX_HEAD_DIM ** -0.5 * LOG2E, jnp.float32), ones], axis=1)
    hqz_scale = jnp.concatenate([jnp.full((1, hw), HGRN_DIM ** -0.5, jnp.float32), ones], axis=1)
    xq_scale = jnp.full((1, d), (d // XATTN_HEADS) ** -0.5 * LOG2E, jnp.float32)
    d_ones = jnp.ones((1, d), jnp.float32)

    xf = x.reshape(m, d)
    xb = xf.astype(bf)
    memb = mem.reshape(batch * n_mem, d).astype(bf)
    norm_w = hgrn_norm_w.reshape(depth, 1, hw)
    as_par = lambda p: p.reshape(depth, 1, -1)
    ln1 = (as_par(ln1_g), as_par(ln1_b))
    ln2 = (as_par(ln2_g), as_par(ln2_b))
    ln3 = (as_par(ln3_g), as_par(ln3_b))
    conv_b3 = as_par(conv_b)

    for l in range(depth):
        wf = jnp.zeros((d, FGATE_LANES), bf).at[:, :FOX_HEADS].set(w_in[l][:, o_ff:o_hq].astype(bf))
        f_bias = jnp.zeros((1, FGATE_LANES), jnp.float32).at[0, :FOX_HEADS].set(fox_f_bias[l])

        qk = _matmul(xb, w_in, l, o_fq, 2, qk_scale, bf, 1024, tn, "in_proj_qk")
        vt = _matmul_t(w_in, l, o_fv // tn, xb, bf, 1024, tn, "in_proj_vt")
        hqz = _matmul(xb, w_in, l, o_hq, 2, hqz_scale, jnp.float32, 1024, tn, "in_proj_hqz")
        hv = _matmul(xb, w_in, l, o_hi, 1, ones, bf, 1024, tn, "in_proj_hv")
        hg = _matmul(xb, w_in, l, o_hg, 1, ones, jnp.float32, 1024, tn, "in_proj_hg")
        ka, qa = _fox_bias_operands(xb, wf, f_bias, batch, seq)
        fox_out = _fox_attention(qk.reshape(batch, seq, 2 * fw), qa.reshape(batch, seq, fw),
                                 ka.reshape(batch, seq, fw), vt, batch, seq)
        h_out = _hgrn(hqz.reshape(batch, seq, 2 * hw), hg.reshape(batch, seq, hw),
                      hv.reshape(batch, seq, hw), hgrn_lb_logits, norm_w, l, batch, seq)
        xf, xb = _mix_out(fox_out.reshape(m, fw), h_out.reshape(m, hw), w_out, xf, *ln1, l, alpha)

        q = _matmul(xb, xq_w, l, 0, 2, xq_scale, bf, 1024, tn, "xattn_q")
        k = _matmul(memb, xk_w, l, 0, 2, d_ones, bf, 1024, tn, "xattn_k")
        v = _matmul(memb, xv_w, l, 0, 2, d_ones, bf, 1024, tn, "xattn_v")
        xf, xb = _xattn(q, k, v, xo_w, xf, *ln2, l, alpha, batch, seq)

        gated = _ffn_up(xb, ffn_up, conv_w, conv_b3, l, seq)
        xf, xb = _ffn_down(gated, _cast_layer(ffn_down, l), xf, *ln3, l, alpha)
    return xf.reshape(batch, seq, d)
```

```python
import functools
import math

import jax
import jax.numpy as jnp
from jax import lax
from jax.experimental import pallas as pl
from jax.experimental.pallas import tpu as pltpu

FOX_HEADS = 8
FOX_HEAD_DIM = 128
HGRN_HEADS = 8
HGRN_DIM = 128
XATTN_HEADS = 4
CONV_WIDTH = 3
LN_EPS = 1e-5
RMS_EPS = 1e-6
MASK_VALUE = -1e30
MIN_FORGET = 1e-6
LOG2E = 1.4426950408889634

V7X_VMEM_LIMIT_BYTES = 56 * 1024 * 1024
SUBLANES = 8
LANES = 128
BF16_SUBLANES = 16
HGRN_CHUNK = 128
HGRN_DIAG = 8
FGATE_LANES = 128
N_PIECES = 3


def _params(*sem):
    return pltpu.CompilerParams(dimension_semantics=sem,
                                vmem_limit_bytes=V7X_VMEM_LIMIT_BYTES)


def _nt_dot(a, b):
    return lax.dot_general(a, b, (((1,), (1,)), ((), ())),
                           preferred_element_type=jnp.float32)


def _layer_norm_rows(y, g, b):
    mu = jnp.mean(y, axis=-1, keepdims=True)
    d = y - mu
    var = jnp.mean(d * d, axis=-1, keepdims=True)
    return d * lax.rsqrt(var + LN_EPS) * g + b


def _mm_kernel(a_ref, w_ref, s_ref, o_ref, wb_ref):
    @pl.when(pl.program_id(1) == 0)
    def _():
        wb_ref[...] = w_ref[...].astype(wb_ref.dtype)

    acc = jnp.dot(a_ref[...], wb_ref[...], preferred_element_type=jnp.float32)
    o_ref[...] = (acc * s_ref[...]).astype(o_ref.dtype)


def _matmul(a, w, layer, col_scale, out_dtype, tm, tn, name):
    m, k = a.shape
    n = w.shape[2]
    tm = min(tm, m)
    return pl.pallas_call(
        _mm_kernel,
        grid=(n // tn, m // tm),
        in_specs=[pl.BlockSpec((tm, k), lambda j, i: (i, 0)),
                  pl.BlockSpec((None, k, tn), lambda j, i: (layer, 0, j)),
                  pl.BlockSpec((1, tn), lambda j, i: (0, j))],
        out_specs=pl.BlockSpec((tm, tn), lambda j, i: (i, j)),
        out_shape=jax.ShapeDtypeStruct((m, n), out_dtype),
        scratch_shapes=[pltpu.VMEM((k, tn), jnp.bfloat16)],
        compiler_params=_params("arbitrary", "arbitrary"),
        name=name,
    )(a, w, col_scale)


def _mm_nt_kernel(a_ref, w_ref, s_ref, o_ref, wb_ref, *, transpose_out):
    @pl.when(pl.program_id(1) == 0)
    def _():
        wb_ref[...] = w_ref[...].astype(wb_ref.dtype)

    if transpose_out:
        o_ref[...] = _nt_dot(wb_ref[...], a_ref[...]).astype(o_ref.dtype)
    else:
        o_ref[...] = (_nt_dot(a_ref[...], wb_ref[...]) * s_ref[...]).astype(o_ref.dtype)


def _matmul_nt(a, wt, layer, row0, n_blocks, col_scale, out_dtype, tm, tn, name,
               transpose_out=False):
    m, k = a.shape
    tm = min(tm, m)
    assert row0 % SUBLANES == 0
    n = n_blocks * tn
    if transpose_out:
        out_spec = pl.BlockSpec((tn, tm), lambda j, i: (j, i))
        out_shape = jax.ShapeDtypeStruct((n, m), out_dtype)
    else:
        out_spec = pl.BlockSpec((tm, tn), lambda j, i: (i, j))
        out_shape = jax.ShapeDtypeStruct((m, n), out_dtype)
    return pl.pallas_call(
        functools.partial(_mm_nt_kernel, transpose_out=transpose_out),
        grid=(n_blocks, m // tm),
        in_specs=[pl.BlockSpec((tm, k), lambda j, i: (i, 0)),
                  pl.BlockSpec((None, pl.Element(tn), pl.Element(k)),
                               lambda j, i: (layer, pl.multiple_of(row0 + j * tn, SUBLANES), 0)),
                  pl.BlockSpec((1, tn), lambda j, i: (0, j))],
        out_specs=out_spec,
        out_shape=out_shape,
        scratch_shapes=[pltpu.VMEM((tn, k), jnp.bfloat16)],
        compiler_params=_params("arbitrary", "arbitrary"),
        name=name,
    )(a, wt, col_scale)


def _cast_kernel(w_ref, o_ref):
    o_ref[...] = w_ref[...].astype(o_ref.dtype)


def _cast_layer(w, layer, rows=512):
    _, k, n = w.shape
    return pl.pallas_call(
        _cast_kernel,
        grid=(k // rows,),
        in_specs=[pl.BlockSpec((None, rows, n), lambda r: (layer, r, 0))],
        out_specs=pl.BlockSpec((rows, n), lambda r: (r, 0)),
        out_shape=jax.ShapeDtypeStruct((k, n), jnp.bfloat16),
        compiler_params=_params("arbitrary"),
        name="cast_layer_bf16",
    )(w)


def _fgate_kernel(x_ref, wt_ref, b_ref, selk_ref, selq_ref, onek_ref, oneq_ref,
                  ka_ref, qa_ref, carry_ref, wf_ref, *, ts, heads):
    @pl.when(pl.program_id(1) == 0)
    def _():
        carry_ref[...] = jnp.zeros_like(carry_ref)
        w = wt_ref[...]
        row = lax.broadcasted_iota(jnp.int32, w.shape, 0)
        wf_ref[...] = jnp.where(row < heads, w, 0.0).astype(wf_ref.dtype)

    z = _nt_dot(x_ref[...], wf_ref[...]) + b_ref[...]
    lf = jnp.minimum(z, 0.0) - jnp.log(1.0 + jnp.exp(-jnp.abs(z)))
    row = lax.broadcasted_iota(jnp.int32, lf.shape, 0)
    k = 1
    while k < ts:
        lf = lf + jnp.where(row >= k, pltpu.roll(lf, k, axis=0), 0.0)
        k *= 2
    c = lf + carry_ref[0:1, :]
    carry_ref[...] = jnp.broadcast_to(c[ts - 1:ts, :], carry_ref.shape)
    c2 = c * LOG2E
    hi = c2.astype(jnp.bfloat16)
    r1 = c2 - hi.astype(jnp.float32)
    mid = r1.astype(jnp.bfloat16)
    lo = (r1 - mid.astype(jnp.float32)).astype(jnp.bfloat16)
    pieces = jnp.concatenate([hi, mid, lo], axis=1)
    ka = jnp.dot(pieces, selk_ref[...], preferred_element_type=jnp.float32) + onek_ref[...]
    qa = jnp.dot(pieces, selq_ref[...], preferred_element_type=jnp.float32) + oneq_ref[...]
    ka_ref[...] = ka.astype(ka_ref.dtype)
    qa_ref[...] = qa.astype(qa_ref.dtype)


def _fox_bias_operands(xb, wt, layer, row0, bias, batch, seq, ts=512):
    m, d = xb.shape
    ts = min(ts, seq)
    nt = seq // ts
    lanes = FGATE_LANES
    heads = FOX_HEADS
    width = heads * FOX_HEAD_DIM
    piece = jnp.arange(N_PIECES)
    head = jnp.arange(heads)
    rows = (piece[None, :] * lanes + head[:, None]).reshape(-1)
    cols_k = (head[:, None] * FOX_HEAD_DIM + piece[None, :]).reshape(-1)
    selk = jnp.zeros((N_PIECES * lanes, width), jnp.float32).at[rows, cols_k].set(-1.0)
    selq = jnp.zeros((N_PIECES * lanes, width), jnp.float32).at[rows, cols_k + N_PIECES].set(1.0)
    onek = jnp.zeros((1, width), jnp.float32).at[0, cols_k + N_PIECES].set(1.0)
    oneq = jnp.zeros((1, width), jnp.float32).at[0, cols_k].set(1.0)
    fixed = lambda b, j: (0, 0)
    row = lambda b, j: (b * nt + j, 0)
    return pl.pallas_call(
        functools.partial(_fgate_kernel, ts=ts, heads=heads),
        grid=(batch, nt),
        in_specs=[pl.BlockSpec((ts, d), row),
                  pl.BlockSpec((None, pl.Element(lanes), pl.Element(d)),
                               lambda b, j: (layer, row0, 0)),
                  pl.BlockSpec((1, lanes), fixed),
                  pl.BlockSpec((N_PIECES * lanes, width), fixed),
                  pl.BlockSpec((N_PIECES * lanes, width), fixed),
                  pl.BlockSpec((1, width), fixed),
                  pl.BlockSpec((1, width), fixed)],
        out_specs=[pl.BlockSpec((ts, width), row), pl.BlockSpec((ts, width), row)],
        out_shape=[jax.ShapeDtypeStruct((m, width), jnp.bfloat16),
                   jax.ShapeDtypeStruct((m, width), jnp.bfloat16)],
        scratch_shapes=[pltpu.VMEM((SUBLANES, lanes), jnp.float32),
                        pltpu.VMEM((lanes, d), jnp.bfloat16)],
        compiler_params=_params("arbitrary", "arbitrary"),
        name="fox_bias_operands",
    )(xb, wt, bias, selk.astype(jnp.bfloat16), selq.astype(jnp.bfloat16), onek, oneq)


def _fox_kernel(q_ref, qa_ref, k_ref, ka_ref, vt_ref, o_ref, *score_refs, tq, tk, sub):
    sc0_ref = score_refs[:len(score_refs) // 2]
    sc1_ref = score_refs[len(score_refs) // 2:]
    i = pl.program_id(2)
    dh = q_ref.shape[1]
    nsub = tq // sub
    q_ext = [jnp.concatenate([q_ref[s * sub:(s + 1) * sub, :], qa_ref[s * sub:(s + 1) * sub, :]],
                             axis=1) for s in range(nsub)]

    def kv_block(k0, n):
        k_ext = jnp.concatenate([k_ref[pl.ds(k0, n), :], ka_ref[pl.ds(k0, n), :]], axis=1)
        return k_ext, vt_ref[:, pl.ds(k0, n)]

    def probs(carry, s):
        m, _ = carry
        m_new = jnp.maximum(m, jnp.max(s, axis=0, keepdims=True))
        p = jnp.exp2(s - m_new).astype(jnp.bfloat16)
        return m_new, jnp.exp2(m - m_new), p

    def accumulate(carry, soft, vt):
        m_new, alpha, p = soft
        vt_ones = jnp.concatenate([vt, jnp.ones((BF16_SUBLANES, vt.shape[1]), vt.dtype)], axis=0)
        acc = alpha * carry[1] + jnp.dot(vt_ones, p, preferred_element_type=jnp.float32)
        return m_new, acc

    def update(carry, s, vt):
        return accumulate(carry, probs(carry, s), vt)

    def put_scores(j, sc_ref):
        k_ext, _ = kv_block(pl.multiple_of(j * tk, tk), tk)
        for s in range(nsub):
            sc_ref[s][...] = _nt_dot(k_ext, q_ext[s])

    def soft_block(sc_ref, carries):
        return [probs(c, sc_ref[s][...]) for s, c in enumerate(carries)]

    def pv_block(j, softs, carries):
        _, vt = kv_block(pl.multiple_of(j * tk, tk), tk)
        return tuple(accumulate(c, soft, vt) for c, soft in zip(carries, softs))

    def causal(sc, k_lo, q_lo):
        key = lax.broadcasted_iota(jnp.int32, sc.shape, 0) + k_lo
        qry = lax.broadcasted_iota(jnp.int32, sc.shape, 1) + q_lo
        return jnp.where(key <= qry, sc, MASK_VALUE)

    def body(jj, carries):
        j = 2 * jj
        put_scores(j + 1, sc1_ref)
        soft = soft_block(sc0_ref, carries)
        put_scores(j + 2, sc0_ref)
        carries = pv_block(j, soft, carries)
        soft = soft_block(sc1_ref, carries)
        return pv_block(j + 1, soft, carries)

    init = tuple((jnp.full((1, sub), MASK_VALUE, jnp.float32),
                  jnp.zeros((dh + BF16_SUBLANES, sub), jnp.float32)) for _ in range(nsub))
    put_scores(0, sc0_ref)
    carries = list(lax.fori_loop(0, i * (tq // tk // 2), body, init))

    base = i * tq
    _, vt = kv_block(pl.multiple_of(base, tk), tk)
    for s in range(nsub):
        sc = sc0_ref[s][...] if tk - 1 <= s * sub else causal(sc0_ref[s][...], 0, s * sub)
        carries[s] = update(carries[s], sc, vt)
    for kb in range(1, tq // tk):
        for s in range(nsub):
            q_lo, q_hi = s * sub, (s + 1) * sub
            k_lo = kb * tk
            k_hi = min(k_lo + tk, q_hi)
            if k_hi <= k_lo:
                continue
            k_ext, vt = kv_block(pl.multiple_of(base + k_lo, sub), k_hi - k_lo)
            sc = _nt_dot(k_ext, q_ext[s])
            if k_hi - 1 > q_lo:
                sc = causal(sc, k_lo, q_lo)
            carries[s] = update(carries[s], sc, vt)
    out = jnp.concatenate([acc[:dh] / acc[dh:dh + 1] for _, acc in carries], axis=1)
    o_ref[...] = out.T.astype(o_ref.dtype)


def _fox_attention(proj, qa, ka, vt, batch, seq, tq=1024, tk=512, sub=256):
    tq = min(tq, seq)
    assert tq % (2 * tk) == 0 and tk % sub == 0
    dh = FOX_HEAD_DIM
    h = FOX_HEADS
    score_bufs = [pltpu.VMEM((tk, sub), jnp.float32)] * (2 * (tq // sub))
    q_map = lambda b, hh, i: (b, i, hh)
    k_map = lambda b, hh, i: (b, 0, hh)
    return pl.pallas_call(
        functools.partial(_fox_kernel, tq=tq, tk=tk, sub=sub),
        grid=(batch, h, seq // tq),
        in_specs=[pl.BlockSpec((None, tq, dh), q_map),
                  pl.BlockSpec((None, tq, dh), q_map),
                  pl.BlockSpec((None, seq, dh), lambda b, hh, i: (b, 0, h + hh)),
                  pl.BlockSpec((None, seq, dh), k_map),
                  pl.BlockSpec((dh, seq), lambda b, hh, i: (hh, b))],
        out_specs=pl.BlockSpec((None, tq, dh), q_map),
        out_shape=jax.ShapeDtypeStruct((batch, seq, h * dh), jnp.bfloat16),
        scratch_shapes=score_bufs,
        compiler_params=_params("arbitrary", "arbitrary", "arbitrary"),
        name="fox_attention",
    )(proj, qa, proj, ka, vt)


def _group_ref(g, group, ref_row):
    c, d = g.shape
    g3 = g.reshape(c // group, group, d)
    return jnp.broadcast_to(g3[:, ref_row:ref_row + 1, :], g3.shape).reshape(c, d)


def _hgrn_kernel(q_ref, z_ref, g_ref, v_ref, lbl_ref, nw_ref, o_ref, st_ref,
                 *, layer, rows, chunk):
    @pl.when(pl.program_id(2) == 0)
    def _():
        st_ref[...] = jnp.zeros_like(st_ref)

    lg = lbl_ref[...]
    e = jnp.exp(lg - jnp.max(lg, axis=0, keepdims=True))
    soft = e / jnp.sum(e, axis=0, keepdims=True)
    cs = soft[0:1]
    for idx in range(1, layer + 1):
        cs = cs + soft[idx:idx + 1]
    lb = cs - soft[0:1]

    z = z_ref[...]
    ez = jnp.exp(-jnp.abs(z))
    r = 1.0 / (1.0 + ez)
    pos = z >= 0.0
    sig = jnp.where(pos, r, ez * r)
    nsig = jnp.where(pos, ez * r, r)
    f_gate = lb + (1.0 - lb) * sig
    logf = jnp.log(jnp.maximum(f_gate, MIN_FORGET))
    k_in = (1.0 - lb) * nsig

    rt = lax.broadcasted_iota(jnp.int32, (chunk, chunk), 0)
    cc = lax.broadcasted_iota(jnp.int32, (chunk, chunk), 1)
    diag_mask = ((rt // HGRN_DIAG) == (cc // HGRN_DIAG)) & (cc <= rt)
    tri_r = lax.broadcasted_iota(jnp.int32, (chunk, N_PIECES * chunk), 0)
    tri_c = lax.broadcasted_iota(jnp.int32, (chunk, N_PIECES * chunk), 1) & (chunk - 1)
    tri = jnp.where(tri_c <= tri_r, 1.0, 0.0).astype(jnp.bfloat16)
    zeros = {}
    span = chunk // 2
    while span >= HGRN_DIAG:
        zeros[span] = jnp.zeros((chunk // (2 * span), span, HGRN_DIM), jnp.float32)
        span //= 2

    nw = nw_ref[...]
    n_chunks = rows // chunk
    slices = [slice(c * chunk, (c + 1) * chunk) for c in range(n_chunks)]

    hi = logf.astype(jnp.bfloat16)
    r1 = logf - hi.astype(jnp.float32)
    mid = r1.astype(jnp.bfloat16)
    lo = (r1 - mid.astype(jnp.float32)).astype(jnp.bfloat16)
    gcs = [jnp.dot(tri, jnp.concatenate([hi[sl], mid[sl], lo[sl]], axis=0),
                   preferred_element_type=jnp.float32) for sl in slices]

    intra = []
    for c, sl in enumerate(slices):
        gc = gcs[c]
        qc = q_ref[sl, :]
        kc = k_in[sl]

        a = None
        span = chunk // 2
        while span >= HGRN_DIAG:
            group = 2 * span
            ng = chunk // group
            g3 = gc.reshape(ng, group, HGRN_DIM)
            gref = g3[:, span:span + 1, :]
            q_up = qc.reshape(ng, group, HGRN_DIM)[:, span:, :] * jnp.exp(g3[:, span:, :] - gref)
            k_lo = kc.reshape(ng, group, HGRN_DIM)[:, :span, :] * jnp.exp(gref - g3[:, :span, :])
            ql = jnp.concatenate([zeros[span], q_up], axis=1).reshape(chunk, HGRN_DIM)
            kl = jnp.concatenate([k_lo, zeros[span]], axis=1).reshape(chunk, HGRN_DIM)
            p = _nt_dot(ql.astype(jnp.bfloat16), kl.astype(jnp.bfloat16))
            a = p if a is None else jnp.where((rt // group) == (cc // group), p, a)
            span //= 2
        dref = gc - _group_ref(gc, HGRN_DIAG, HGRN_DIAG // 2)
        p = _nt_dot((qc * jnp.exp(dref)).astype(jnp.bfloat16),
                    (kc * jnp.exp(-dref)).astype(jnp.bfloat16))
        intra.append(jnp.where(diag_mask, p, a).astype(jnp.bfloat16))

    o_intra, upds, qgs, decs = [], [], [], []
    for c, sl in enumerate(slices):
        gc = gcs[c]
        vc = v_ref[sl, :]
        g_last = gc[chunk - 1:chunk]
        k_dec = (k_in[sl] * jnp.exp(g_last - gc)).astype(jnp.bfloat16)
        o_intra.append(jnp.dot(intra[c], vc, preferred_element_type=jnp.float32))
        upds.append(lax.dot_general(vc, k_dec, (((0,), (0,)), ((), ())),
                                    preferred_element_type=jnp.float32))
        qgs.append((q_ref[sl, :] * jnp.exp(gc)).astype(jnp.bfloat16))
        decs.append(jnp.exp(g_last))

    st = st_ref[...]
    for c, sl in enumerate(slices):
        o = o_intra[c] + _nt_dot(qgs[c], st.astype(jnp.bfloat16))
        st = st * decs[c] + upds[c]

        ms = jnp.mean(o * o, axis=-1, keepdims=True)
        y = o * lax.rsqrt(ms + RMS_EPS) * nw
        gate = g_ref[sl, :]
        eg = jnp.exp(-jnp.abs(gate))
        rg = 1.0 / (1.0 + eg)
        y = y * (gate * jnp.where(gate >= 0.0, rg, eg * rg))
        o_ref[sl, :] = y.astype(o_ref.dtype)
    st_ref[...] = st


def _hgrn(hqz, hg, hv, lb_logits, norm_w, layer, batch, seq, rows=512):
    rows = min(rows, seq)
    h = HGRN_HEADS
    d = HGRN_DIM
    depth = lb_logits.shape[0]
    head = lambda b, hh, r: (b, r, hh)
    return pl.pallas_call(
        functools.partial(_hgrn_kernel, layer=layer, rows=rows, chunk=HGRN_CHUNK),
        grid=(batch, h, seq // rows),
        in_specs=[pl.BlockSpec((None, rows, d), head),
                  pl.BlockSpec((None, rows, d), lambda b, hh, r: (b, r, h + hh)),
                  pl.BlockSpec((None, rows, d), head),
                  pl.BlockSpec((None, rows, d), head),
                  pl.BlockSpec((depth, d), lambda b, hh, r: (0, hh)),
                  pl.BlockSpec((None, 1, d), lambda b, hh, r: (layer, 0, hh))],
        out_specs=pl.BlockSpec((None, rows, d), head),
        out_shape=jax.ShapeDtypeStruct((batch, seq, h * d), jnp.bfloat16),
        scratch_shapes=[pltpu.VMEM((d, d), jnp.float32)],
        compiler_params=_params("arbitrary", "arbitrary", "arbitrary"),
        name="hgrn2",
    )(hqz, hqz, hg, hv, lb_logits, norm_w)


def _mix_out_kernel(a1_ref, a2_ref, w1_ref, w2_ref, x_ref, g_ref, b_ref, o_ref, ob_ref,
                    wb_ref, *, alpha):
    @pl.when(pl.program_id(0) == 0)
    def _():
        wb_ref[0] = w1_ref[...].astype(wb_ref.dtype)
        wb_ref[1] = w2_ref[...].astype(wb_ref.dtype)

    acc = jnp.dot(a1_ref[...], wb_ref[0], preferred_element_type=jnp.float32)
    acc = acc + jnp.dot(a2_ref[...], wb_ref[1], preferred_element_type=jnp.float32)
    o = _layer_norm_rows(alpha * x_ref[...] + acc, g_ref[...], b_ref[...])
    o_ref[...] = o
    ob_ref[...] = o.astype(ob_ref.dtype)


def _resident(block_shape, index_map):
    return pl.BlockSpec(block_shape, index_map, pipeline_mode=pl.Buffered(1))


def _mix_out(a1, a2, w, x, g, b, layer, alpha, tm=512):
    m, d = x.shape
    k = a1.shape[1]
    tm = min(tm, m)
    row = lambda i: (i, 0)
    par = lambda i: (layer, 0, 0)
    return pl.pallas_call(
        functools.partial(_mix_out_kernel, alpha=alpha),
        grid=(m // tm,),
        in_specs=[pl.BlockSpec((tm, k), row), pl.BlockSpec((tm, k), row),
                  _resident((None, k, d), par), _resident((None, k, d), lambda i: (layer, 1, 0)),
                  pl.BlockSpec((tm, d), row),
                  _resident((None, 1, d), par), _resident((None, 1, d), par)],
        out_specs=[pl.BlockSpec((tm, d), row), pl.BlockSpec((tm, d), row)],
        out_shape=[jax.ShapeDtypeStruct((m, d), jnp.float32),
                   jax.ShapeDtypeStruct((m, d), jnp.bfloat16)],
        scratch_shapes=[pltpu.VMEM((2, k, d), jnp.bfloat16)],
        compiler_params=_params("arbitrary"),
        name="mix_out_ln",
    )(a1, a2, w, w, x, g, b)


def _xattn_kernel(q_ref, k_ref, v_ref, wo_ref, x_ref, g_ref, b_ref, o_ref, ob_ref,
                  wb_ref, *, alpha, heads):
    @pl.when(pl.program_id(0) == 0)
    def _():
        wb_ref[...] = wo_ref[...].astype(wb_ref.dtype)

    d = q_ref.shape[1]
    dh = d // heads
    outs = []
    scores = [_nt_dot(q_ref[:, h * dh:(h + 1) * dh], k_ref[:, h * dh:(h + 1) * dh])
              for h in range(heads)]
    for h in range(heads):
        sl = slice(h * dh, (h + 1) * dh)
        s = scores[h]
        p = jnp.exp2(s - jnp.max(s, axis=-1, keepdims=True))
        p = p / jnp.sum(p, axis=-1, keepdims=True)
        outs.append(jnp.dot(p.astype(jnp.bfloat16), v_ref[:, sl],
                            preferred_element_type=jnp.float32).astype(jnp.bfloat16))
    att = jnp.concatenate(outs, axis=1)
    acc = jnp.dot(att, wb_ref[...], preferred_element_type=jnp.float32)
    o = _layer_norm_rows(alpha * x_ref[...] + acc, g_ref[...], b_ref[...])
    o_ref[...] = o
    ob_ref[...] = o.astype(ob_ref.dtype)


def _xattn(q, k, v, wo, x, g, b, layer, alpha, batch, seq, tm=256):
    m, d = x.shape
    n_mem = k.shape[0] // batch
    tm = min(tm, seq)
    per_b = seq // tm
    row = lambda i: (i, 0)
    par = lambda i: (layer, 0, 0)
    mem_map = lambda i: (i // per_b, 0)
    return pl.pallas_call(
        functools.partial(_xattn_kernel, alpha=alpha, heads=XATTN_HEADS),
        grid=(m // tm,),
        in_specs=[pl.BlockSpec((tm, d), row),
                  pl.BlockSpec((n_mem, d), mem_map),
                  pl.BlockSpec((n_mem, d), mem_map),
                  _resident((None, d, d), par),
                  pl.BlockSpec((tm, d), row),
                  _resident((None, 1, d), par), _resident((None, 1, d), par)],
        out_specs=[pl.BlockSpec((tm, d), row), pl.BlockSpec((tm, d), row)],
        out_shape=[jax.ShapeDtypeStruct((m, d), jnp.float32),
                   jax.ShapeDtypeStruct((m, d), jnp.bfloat16)],
        scratch_shapes=[pltpu.VMEM((d, d), jnp.bfloat16)],
        compiler_params=_params("arbitrary"),
        name="xattn_out_ln",
    )(q, k, v, wo, x, g, b)


def _ffn_up_kernel(xb_ref, wa_ref, wb_ref, cwa_ref, cwb_ref, cba_ref, cbb_ref, o_ref,
                   wab_ref, halo_ref, *, tiles_per_seq, rs):
    i = pl.program_id(1)
    tm = xb_ref.shape[0]

    @pl.when(i == 0)
    def _():
        wab_ref[0] = wa_ref[...].astype(wab_ref.dtype)
        wab_ref[1] = wb_ref[...].astype(wab_ref.dtype)

    @pl.when(i % tiles_per_seq == 0)
    def _():
        halo_ref[...] = jnp.zeros_like(halo_ref)

    cws = (cwa_ref[...], cwb_ref[...])
    cbs = (cba_ref[...], cbb_ref[...])
    prev = [halo_ref[0], halo_ref[1]]
    for r in range(tm // rs):
        xr = xb_ref[r * rs:(r + 1) * rs, :]
        branches = []
        for t in range(2):
            h = jnp.dot(xr, wab_ref[t], preferred_element_type=jnp.float32)
            seam = jnp.concatenate([prev[t], h[:SUBLANES]], axis=0)
            h1 = jnp.concatenate([seam[SUBLANES - 1:2 * SUBLANES - 1],
                                  pltpu.roll(h, 1, axis=0)[SUBLANES:]], axis=0)
            h2 = jnp.concatenate([seam[SUBLANES - 2:2 * SUBLANES - 2],
                                  pltpu.roll(h, 2, axis=0)[SUBLANES:]], axis=0)
            prev[t] = h[rs - SUBLANES:rs]
            cw = cws[t]
            branches.append(cw[0:1] * h2 + cw[1:2] * h1 + cw[2:3] * h + cbs[t])
        a, bb = branches
        ea = jnp.exp(-jnp.abs(a))
        ra = 1.0 / (1.0 + ea)
        gated = (a * jnp.where(a >= 0.0, ra, ea * ra)) * bb
        o_ref[r * rs:(r + 1) * rs, :] = gated.astype(o_ref.dtype)
    halo_ref[0] = prev[0]
    halo_ref[1] = prev[1]


def _ffn_up(xb, w_up, conv_w, conv_b, layer, seq, tm=2048, tf=512, rs=256):
    m, d = xb.shape
    dff = w_up.shape[2] // 2
    tm = min(tm, seq)
    nj = dff // tf
    col_a = lambda j, i: (layer, 0, j)
    col_b = lambda j, i: (layer, 0, nj + j)
    return pl.pallas_call(
        functools.partial(_ffn_up_kernel, tiles_per_seq=seq // tm, rs=rs),
        grid=(nj, m // tm),
        in_specs=[pl.BlockSpec((tm, d), lambda j, i: (i, 0)),
                  pl.BlockSpec((None, d, tf), col_a), pl.BlockSpec((None, d, tf), col_b),
                  pl.BlockSpec((None, CONV_WIDTH, tf), col_a),
                  pl.BlockSpec((None, CONV_WIDTH, tf), col_b),
                  pl.BlockSpec((None, 1, tf), col_a), pl.BlockSpec((None, 1, tf), col_b)],
        out_specs=pl.BlockSpec((tm, tf), lambda j, i: (i, j)),
        out_shape=jax.ShapeDtypeStruct((m, dff), jnp.bfloat16),
        scratch_shapes=[pltpu.VMEM((2, d, tf), jnp.bfloat16),
                        pltpu.VMEM((2, SUBLANES, tf), jnp.float32)],
        compiler_params=_params("arbitrary", "arbitrary"),
        name="ffn_up_conv_gate",
    )(xb, w_up, w_up, conv_w, conv_w, conv_b, conv_b)


def _ffn_down_kernel(a_ref, w_ref, x_ref, g_ref, b_ref, o_ref, ob_ref, *, alpha):
    acc = jnp.dot(a_ref[...], w_ref[...], preferred_element_type=jnp.float32)
    o = _layer_norm_rows(alpha * x_ref[...] + acc, g_ref[...], b_ref[...])
    o_ref[...] = o
    ob_ref[...] = o.astype(ob_ref.dtype)


def _ffn_down(a, w, x, g, b, layer, alpha, tm=256):
    m, d = x.shape
    f = a.shape[1]
    tm = min(tm, m)
    row = lambda i: (i, 0)
    par = lambda i: (layer, 0, 0)
    return pl.pallas_call(
        functools.partial(_ffn_down_kernel, alpha=alpha),
        grid=(m // tm,),
        in_specs=[pl.BlockSpec((tm, f), row),
                  _resident((f, d), lambda i: (0, 0)),
                  pl.BlockSpec((tm, d), row),
                  _resident((None, 1, d), par), _resident((None, 1, d), par)],
        out_specs=[pl.BlockSpec((tm, d), row), pl.BlockSpec((tm, d), row)],
        out_shape=[jax.ShapeDtypeStruct((m, d), jnp.float32),
                   jax.ShapeDtypeStruct((m, d), jnp.bfloat16)],
        compiler_params=_params("arbitrary"),
        name="ffn_down_ln",
    )(a, w, x, g, b)


def kernel(x, mem, w_in, fox_f_bias, hgrn_lb_logits, hgrn_norm_w, w_out, ln1_g, ln1_b,
           xq_w, xk_w, xv_w, xo_w, ln2_g, ln2_b, ffn_up, conv_w, conv_b, ffn_down,
           ln3_g, ln3_b):
    batch, seq, d = x.shape
    depth = w_in.shape[0]
    alpha = (2 * depth) ** 0.25
    bf = jnp.bfloat16
    fw = FOX_HEADS * FOX_HEAD_DIM
    hw = HGRN_HEADS * HGRN_DIM
    m = batch * seq
    n_mem = mem.shape[1]

    o_fq, o_fk, o_fv, o_ff = 0, fw, 2 * fw, 3 * fw
    o_hq = o_ff + FOX_HEADS
    o_hf, o_hi, o_hg = o_hq + hw, o_hq + 2 * hw, o_hq + 3 * hw

    tn = 1024
    ones = jnp.ones((1, tn), jnp.float32)
    qk_scale = jnp.concatenate(
        [jnp.full((1, fw), FOX_HEAD_DIM ** -0.5 * LOG2E, jnp.float32), ones], axis=1)
    hqz_scale = jnp.concatenate([jnp.full((1, hw), HGRN_DIM ** -0.5, jnp.float32), ones], axis=1)
    xq_scale = jnp.full((1, d), (d // XATTN_HEADS) ** -0.5 * LOG2E, jnp.float32)
    d_ones = jnp.ones((1, d), jnp.float32)

    xf = x.reshape(m, d)
    xb = xf.astype(bf)
    memb = mem.reshape(batch * n_mem, d).astype(bf)
    norm_w = hgrn_norm_w.reshape(depth, 1, hw)
    as_par = lambda p: p.reshape(depth, 1, -1)
    ln1 = (as_par(ln1_g), as_par(ln1_b))
    ln2 = (as_par(ln2_g), as_par(ln2_b))
    ln3 = (as_par(ln3_g), as_par(ln3_b))
    conv_b3 = as_par(conv_b)

    w_in_t = jnp.swapaxes(w_in, 1, 2)

    for l in range(depth):
        f_bias = jnp.zeros((1, FGATE_LANES), jnp.float32).at[0, :FOX_HEADS].set(fox_f_bias[l])

        qk = _matmul_nt(xb, w_in_t, l, o_fq, 2, qk_scale, bf, 1024, tn, "in_proj_qk")
        vt = _matmul_nt(xb, w_in_t, l, o_fv, 1, ones, bf, 1024, tn, "in_proj_vt",
                        transpose_out=True)
        hqz = _matmul_nt(xb, w_in_t, l, o_hq, 2, hqz_scale, jnp.float32, 1024, tn, "in_proj_hqz")
        hv = _matmul_nt(xb, w_in_t, l, o_hi, 1, ones, bf, 1024, tn, "in_proj_hv")
        hg = _matmul_nt(xb, w_in_t, l, o_hg, 1, ones, jnp.float32, 1024, tn, "in_proj_hg")
        ka, qa = _fox_bias_operands(xb, w_in_t, l, o_ff, f_bias, batch, seq)
        fox_out = _fox_attention(qk.reshape(batch, seq, 2 * fw), qa.reshape(batch, seq, fw),
                                 ka.reshape(batch, seq, fw), vt, batch, seq)
        h_out = _hgrn(hqz.reshape(batch, seq, 2 * hw), hg.reshape(batch, seq, hw),
                      hv.reshape(batch, seq, hw), hgrn_lb_logits, norm_w, l, batch, seq)
        xf, xb = _mix_out(fox_out.reshape(m, fw), h_out.reshape(m, hw), w_out, xf, *ln1, l, alpha)

        q = _matmul(xb, xq_w, l, xq_scale, bf, 1024, tn, "xattn_q")
        k = _matmul(memb, xk_w, l, d_ones, bf, 1024, tn, "xattn_k")
        v = _matmul(memb, xv_w, l, d_ones, bf, 1024, tn, "xattn_v")
        xf, xb = _xattn(q, k, v, xo_w, xf, *ln2, l, alpha, batch, seq)

        gated = _ffn_up(xb, ffn_up, conv_w, conv_b3, l, seq)
        xf, xb = _ffn_down(gated, _cast_layer(ffn_down, l), xf, *ln3, l, alpha)
    return xf.reshape(batch, seq, d)
```

```python
import functools

import jax
import jax.numpy as jnp
import numpy as np
from jax import lax
from jax.experimental import pallas as pl
from jax.experimental.pallas import tpu as pltpu

FOX_HEADS = 8
FOX_HEAD_DIM = 128
HGRN_HEADS = 8
HGRN_DIM = 128
XATTN_HEADS = 4
CONV_WIDTH = 3
LN_EPS = 1e-5
RMS_EPS = 1e-6
MASK_VALUE = -1e30
MIN_FORGET = 1e-6
LOG2E = 1.4426950408889634

V7X_VMEM_LIMIT_BYTES = 56 * 1024 * 1024
SUBLANES = 8
BF16_SUBLANES = 16
HGRN_CHUNK = 128
HGRN_DIAG = 8
LN_SUB_ROWS = 128
FGATE_LANES = 128
N_PIECES = 3


def _params(*sem):
    return pltpu.CompilerParams(dimension_semantics=sem,
                                vmem_limit_bytes=V7X_VMEM_LIMIT_BYTES)


def _nt_dot(a, b):
    return lax.dot_general(a, b, (((1,), (1,)), ((), ())),
                           preferred_element_type=jnp.float32)


def _row_slices(rows, sub=LN_SUB_ROWS):
    sub = min(sub, rows)
    return [slice(r, r + sub) for r in range(0, rows, sub)]


def _layer_norm_rows(y, g, b):
    mu = jnp.mean(y, axis=-1, keepdims=True)
    d = y - mu
    var = jnp.mean(d * d, axis=-1, keepdims=True)
    return d * lax.rsqrt(var + LN_EPS) * g + b


def _mm_kernel(a_ref, w_ref, s_ref, o_ref, wb_ref):
    @pl.when(pl.program_id(1) == 0)
    def _():
        wb_ref[...] = w_ref[...].astype(wb_ref.dtype)

    acc = jnp.dot(a_ref[...], wb_ref[...], preferred_element_type=jnp.float32)
    o_ref[...] = (acc * s_ref[...]).astype(o_ref.dtype)


def _matmul(a, w, layer, col_scale, out_dtype, tm, tn, name):
    m, k = a.shape
    n = w.shape[2]
    tm = min(tm, m)
    return pl.pallas_call(
        _mm_kernel,
        grid=(n // tn, m // tm),
        in_specs=[pl.BlockSpec((tm, k), lambda j, i: (i, 0)),
                  pl.BlockSpec((None, k, tn), lambda j, i: (layer, 0, j)),
                  pl.BlockSpec((1, tn), lambda j, i: (0, j))],
        out_specs=pl.BlockSpec((tm, tn), lambda j, i: (i, j)),
        out_shape=jax.ShapeDtypeStruct((m, n), out_dtype),
        scratch_shapes=[pltpu.VMEM((k, tn), jnp.bfloat16)],
        compiler_params=_params("arbitrary", "arbitrary"),
        name=name,
    )(a, w, col_scale)


def _mm_nt_kernel(a_ref, w_ref, s_ref, o_ref, wb_ref, *, transpose_out):
    @pl.when(pl.program_id(1) == 0)
    def _():
        wb_ref[...] = w_ref[...].astype(wb_ref.dtype)

    if transpose_out:
        o_ref[...] = _nt_dot(wb_ref[...], a_ref[...]).astype(o_ref.dtype)
    else:
        o_ref[...] = (_nt_dot(a_ref[...], wb_ref[...]) * s_ref[...]).astype(o_ref.dtype)


def _matmul_nt(a, wt, layer, row0, n_blocks, col_scale, out_dtype, tm, tn, name,
               transpose_out=False):
    m, k = a.shape
    tm = min(tm, m)
    assert row0 % SUBLANES == 0
    n = n_blocks * tn
    if transpose_out:
        out_spec = pl.BlockSpec((tn, tm), lambda j, i: (j, i))
        out_shape = jax.ShapeDtypeStruct((n, m), out_dtype)
    else:
        out_spec = pl.BlockSpec((tm, tn), lambda j, i: (i, j))
        out_shape = jax.ShapeDtypeStruct((m, n), out_dtype)
    return pl.pallas_call(
        functools.partial(_mm_nt_kernel, transpose_out=transpose_out),
        grid=(n_blocks, m // tm),
        in_specs=[pl.BlockSpec((tm, k), lambda j, i: (i, 0)),
                  pl.BlockSpec((None, pl.Element(tn), pl.Element(k)),
                               lambda j, i: (layer, pl.multiple_of(row0 + j * tn, SUBLANES), 0)),
                  pl.BlockSpec((1, tn), lambda j, i: (0, j))],
        out_specs=out_spec,
        out_shape=out_shape,
        scratch_shapes=[pltpu.VMEM((tn, k), jnp.bfloat16)],
        compiler_params=_params("arbitrary", "arbitrary"),
        name=name,
    )(a, wt, col_scale)


def _cast_kernel(w_ref, o_ref):
    o_ref[...] = w_ref[...].astype(o_ref.dtype)


def _cast_layer(w, layer, rows=512):
    _, k, n = w.shape
    return pl.pallas_call(
        _cast_kernel,
        grid=(k // rows,),
        in_specs=[pl.BlockSpec((None, rows, n), lambda r: (layer, r, 0))],
        out_specs=pl.BlockSpec((rows, n), lambda r: (r, 0)),
        out_shape=jax.ShapeDtypeStruct((k, n), jnp.bfloat16),
        compiler_params=_params("arbitrary"),
        name="cast_layer_bf16",
    )(w)


def _fgate_kernel(x_ref, wt_ref, b_ref, selk_ref, selq_ref, onek_ref, oneq_ref,
                  ka_ref, qa_ref, carry_ref, wf_ref, *, ts, heads):
    @pl.when(pl.program_id(1) == 0)
    def _():
        carry_ref[...] = jnp.zeros_like(carry_ref)
        w = wt_ref[...]
        row = lax.broadcasted_iota(jnp.int32, w.shape, 0)
        wf_ref[...] = jnp.where(row < heads, w, 0.0).astype(wf_ref.dtype)

    z = _nt_dot(x_ref[...], wf_ref[...]) + b_ref[...]
    lf = jnp.minimum(z, 0.0) - jnp.log(1.0 + jnp.exp(-jnp.abs(z)))
    row = lax.broadcasted_iota(jnp.int32, lf.shape, 0)
    k = 1
    while k < ts:
        lf = lf + jnp.where(row >= k, pltpu.roll(lf, k, axis=0), 0.0)
        k *= 2
    c = lf + carry_ref[0:1, :]
    carry_ref[...] = jnp.broadcast_to(c[ts - 1:ts, :], carry_ref.shape)
    c2 = c * LOG2E
    hi = c2.astype(jnp.bfloat16)
    r1 = c2 - hi.astype(jnp.float32)
    mid = r1.astype(jnp.bfloat16)
    lo = (r1 - mid.astype(jnp.float32)).astype(jnp.bfloat16)
    pieces = jnp.concatenate([hi, mid, lo], axis=1)
    ka = jnp.dot(pieces, selk_ref[...], preferred_element_type=jnp.float32) + onek_ref[...]
    qa = jnp.dot(pieces, selq_ref[...], preferred_element_type=jnp.float32) + oneq_ref[...]
    ka_ref[...] = ka.astype(ka_ref.dtype)
    qa_ref[...] = qa.astype(qa_ref.dtype)


def _fox_bias_operands(xb, wt, layer, row0, bias, batch, seq, ts=512):
    m, d = xb.shape
    ts = min(ts, seq)
    nt = seq // ts
    lanes = FGATE_LANES
    heads = FOX_HEADS
    width = heads * FOX_HEAD_DIM
    piece = np.arange(N_PIECES)
    head = np.arange(heads)
    rows = (piece[None, :] * lanes + head[:, None]).reshape(-1)
    cols_k = (head[:, None] * FOX_HEAD_DIM + piece[None, :]).reshape(-1)
    selk = np.zeros((N_PIECES * lanes, width), np.float32)
    selq = np.zeros((N_PIECES * lanes, width), np.float32)
    onek = np.zeros((1, width), np.float32)
    oneq = np.zeros((1, width), np.float32)
    selk[rows, cols_k] = -1.0
    selq[rows, cols_k + N_PIECES] = 1.0
    onek[0, cols_k + N_PIECES] = 1.0
    oneq[0, cols_k] = 1.0
    fixed = lambda b, j: (0, 0)
    row = lambda b, j: (b * nt + j, 0)
    return pl.pallas_call(
        functools.partial(_fgate_kernel, ts=ts, heads=heads),
        grid=(batch, nt),
        in_specs=[pl.BlockSpec((ts, d), row),
                  pl.BlockSpec((None, pl.Element(lanes), pl.Element(d)),
                               lambda b, j: (layer, row0, 0)),
                  pl.BlockSpec((1, lanes), fixed),
                  pl.BlockSpec((N_PIECES * lanes, width), fixed),
                  pl.BlockSpec((N_PIECES * lanes, width), fixed),
                  pl.BlockSpec((1, width), fixed),
                  pl.BlockSpec((1, width), fixed)],
        out_specs=[pl.BlockSpec((ts, width), row), pl.BlockSpec((ts, width), row)],
        out_shape=[jax.ShapeDtypeStruct((m, width), jnp.bfloat16),
                   jax.ShapeDtypeStruct((m, width), jnp.bfloat16)],
        scratch_shapes=[pltpu.VMEM((SUBLANES, lanes), jnp.float32),
                        pltpu.VMEM((lanes, d), jnp.bfloat16)],
        compiler_params=_params("arbitrary", "arbitrary"),
        name="fox_bias_operands",
    )(xb, wt, bias, jnp.asarray(selk, jnp.bfloat16), jnp.asarray(selq, jnp.bfloat16),
      jnp.asarray(onek), jnp.asarray(oneq))


def _fox_kernel(q_ref, qa_ref, k_ref, ka_ref, vt_ref, o_ref, *score_refs, tq, tk, sub):
    sc0_ref = score_refs[:len(score_refs) // 2]
    sc1_ref = score_refs[len(score_refs) // 2:]
    i = pl.program_id(2)
    dh = q_ref.shape[1]
    nsub = tq // sub
    q_ext = [jnp.concatenate([q_ref[s * sub:(s + 1) * sub, :], qa_ref[s * sub:(s + 1) * sub, :]],
                             axis=1) for s in range(nsub)]

    def kv_block(k0, n):
        k_ext = jnp.concatenate([k_ref[pl.ds(k0, n), :], ka_ref[pl.ds(k0, n), :]], axis=1)
        return k_ext, vt_ref[:, pl.ds(k0, n)]

    def probs(carry, s):
        m, _ = carry
        m_new = jnp.maximum(m, jnp.max(s, axis=0, keepdims=True))
        p = jnp.exp2(s - m_new).astype(jnp.bfloat16)
        return m_new, jnp.exp2(m - m_new), p

    def accumulate(carry, soft, vt):
        m_new, alpha, p = soft
        vt_ones = jnp.concatenate([vt, jnp.ones((BF16_SUBLANES, vt.shape[1]), vt.dtype)], axis=0)
        acc = alpha * carry[1] + jnp.dot(vt_ones, p, preferred_element_type=jnp.float32)
        return m_new, acc

    def update(carry, s, vt):
        return accumulate(carry, probs(carry, s), vt)

    def put_scores(j, sc_ref):
        k_ext, _ = kv_block(pl.multiple_of(j * tk, tk), tk)
        for s in range(nsub):
            sc_ref[s][...] = _nt_dot(k_ext, q_ext[s])

    def soft_block(sc_ref, carries):
        return [probs(c, sc_ref[s][...]) for s, c in enumerate(carries)]

    def pv_block(j, softs, carries):
        _, vt = kv_block(pl.multiple_of(j * tk, tk), tk)
        return tuple(accumulate(c, soft, vt) for c, soft in zip(carries, softs))

    def causal(sc, k_lo, q_lo):
        key = lax.broadcasted_iota(jnp.int32, sc.shape, 0) + k_lo
        qry = lax.broadcasted_iota(jnp.int32, sc.shape, 1) + q_lo
        return jnp.where(key <= qry, sc, MASK_VALUE)

    def body(jj, carries):
        j = 2 * jj
        put_scores(j + 1, sc1_ref)
        soft = soft_block(sc0_ref, carries)
        put_scores(j + 2, sc0_ref)
        carries = pv_block(j, soft, carries)
        soft = soft_block(sc1_ref, carries)
        return pv_block(j + 1, soft, carries)

    init = tuple((jnp.full((1, sub), MASK_VALUE, jnp.float32),
                  jnp.zeros((dh + BF16_SUBLANES, sub), jnp.float32)) for _ in range(nsub))
    put_scores(0, sc0_ref)
    carries = list(lax.fori_loop(0, i * (tq // tk // 2), body, init))

    base = i * tq
    later = []
    for kb in range(1, tq // tk):
        for s in range(nsub):
            q_lo, q_hi = s * sub, (s + 1) * sub
            k_lo = kb * tk
            k_hi = min(k_lo + tk, q_hi)
            if k_hi <= k_lo:
                continue
            k_ext, vt = kv_block(pl.multiple_of(base + k_lo, sub), k_hi - k_lo)
            sc = _nt_dot(k_ext, q_ext[s])
            later.append((s, causal(sc, k_lo, q_lo) if k_hi - 1 > q_lo else sc, vt))
    _, vt = kv_block(pl.multiple_of(base, tk), tk)
    for s in range(nsub):
        sc = sc0_ref[s][...] if tk - 1 <= s * sub else causal(sc0_ref[s][...], 0, s * sub)
        carries[s] = update(carries[s], sc, vt)
    for s, sc, vt in later:
        carries[s] = update(carries[s], sc, vt)
    out = jnp.concatenate([acc[:dh] / acc[dh:dh + 1] for _, acc in carries], axis=1)
    o_ref[...] = out.T.astype(o_ref.dtype)


def _fox_attention(proj, qa, ka, vt, batch, seq, tq=1024, tk=512, sub=256):
    tq = min(tq, seq)
    assert tq % (2 * tk) == 0 and tk % sub == 0
    dh = FOX_HEAD_DIM
    h = FOX_HEADS
    score_bufs = [pltpu.VMEM((tk, sub), jnp.float32)] * (2 * (tq // sub))
    q_map = lambda b, hh, i: (b, i, hh)
    k_map = lambda b, hh, i: (b, 0, hh)
    return pl.pallas_call(
        functools.partial(_fox_kernel, tq=tq, tk=tk, sub=sub),
        grid=(batch, h, seq // tq),
        in_specs=[pl.BlockSpec((None, tq, dh), q_map),
                  pl.BlockSpec((None, tq, dh), q_map),
                  pl.BlockSpec((None, seq, dh), lambda b, hh, i: (b, 0, h + hh)),
                  pl.BlockSpec((None, seq, dh), k_map),
                  pl.BlockSpec((dh, seq), lambda b, hh, i: (hh, b))],
        out_specs=pl.BlockSpec((None, tq, dh), q_map),
        out_shape=jax.ShapeDtypeStruct((batch, seq, h * dh), jnp.bfloat16),
        scratch_shapes=score_bufs,
        compiler_params=_params("arbitrary", "arbitrary", "arbitrary"),
        name="fox_attention",
    )(proj, qa, proj, ka, vt)


def _group_ref(g, group, ref_row):
    c, d = g.shape
    g3 = g.reshape(c // group, group, d)
    return jnp.broadcast_to(g3[:, ref_row:ref_row + 1, :], g3.shape).reshape(c, d)


def _hgrn_kernel(q_ref, z_ref, g_ref, v_ref, lbl_ref, nw_ref, o_ref, st_ref,
                 *, layer, rows, chunk):
    @pl.when(pl.program_id(2) == 0)
    def _():
        st_ref[...] = jnp.zeros_like(st_ref)

    lg = lbl_ref[...]
    e = jnp.exp(lg - jnp.max(lg, axis=0, keepdims=True))
    soft = e / jnp.sum(e, axis=0, keepdims=True)
    cs = soft[0:1]
    for idx in range(1, layer + 1):
        cs = cs + soft[idx:idx + 1]
    lb = cs - soft[0:1]

    z = z_ref[...]
    ez = jnp.exp(-jnp.abs(z))
    r = 1.0 / (1.0 + ez)
    pos = z >= 0.0
    sig = jnp.where(pos, r, ez * r)
    nsig = jnp.where(pos, ez * r, r)
    f_gate = lb + (1.0 - lb) * sig
    logf = jnp.log(jnp.maximum(f_gate, MIN_FORGET))
    k_in = (1.0 - lb) * nsig

    rt = lax.broadcasted_iota(jnp.int32, (chunk, chunk), 0)
    cc = lax.broadcasted_iota(jnp.int32, (chunk, chunk), 1)
    diag_mask = ((rt // HGRN_DIAG) == (cc // HGRN_DIAG)) & (cc <= rt)
    tri_r = lax.broadcasted_iota(jnp.int32, (chunk, N_PIECES * chunk), 0)
    tri_c = lax.broadcasted_iota(jnp.int32, (chunk, N_PIECES * chunk), 1) & (chunk - 1)
    tri = jnp.where(tri_c <= tri_r, 1.0, 0.0).astype(jnp.bfloat16)
    zeros = {}
    span = chunk // 2
    while span >= HGRN_DIAG:
        zeros[span] = jnp.zeros((chunk // (2 * span), span, HGRN_DIM), jnp.float32)
        span //= 2

    nw = nw_ref[...]
    n_chunks = rows // chunk
    slices = [slice(c * chunk, (c + 1) * chunk) for c in range(n_chunks)]

    hi = logf.astype(jnp.bfloat16)
    r1 = logf - hi.astype(jnp.float32)
    mid = r1.astype(jnp.bfloat16)
    lo = (r1 - mid.astype(jnp.float32)).astype(jnp.bfloat16)
    gcs = [jnp.dot(tri, jnp.concatenate([hi[sl], mid[sl], lo[sl]], axis=0),
                   preferred_element_type=jnp.float32) for sl in slices]

    intra = []
    for c, sl in enumerate(slices):
        gc = gcs[c]
        qc = q_ref[sl, :]
        kc = k_in[sl]

        a = None
        span = chunk // 2
        while span >= HGRN_DIAG:
            group = 2 * span
            ng = chunk // group
            g3 = gc.reshape(ng, group, HGRN_DIM)
            gref = g3[:, span:span + 1, :]
            q_up = qc.reshape(ng, group, HGRN_DIM)[:, span:, :] * jnp.exp(g3[:, span:, :] - gref)
            k_lo = kc.reshape(ng, group, HGRN_DIM)[:, :span, :] * jnp.exp(gref - g3[:, :span, :])
            ql = jnp.concatenate([zeros[span], q_up], axis=1).reshape(chunk, HGRN_DIM)
            kl = jnp.concatenate([k_lo, zeros[span]], axis=1).reshape(chunk, HGRN_DIM)
            p = _nt_dot(ql.astype(jnp.bfloat16), kl.astype(jnp.bfloat16))
            a = p if a is None else jnp.where((rt // group) == (cc // group), p, a)
            span //= 2
        dref = gc - _group_ref(gc, HGRN_DIAG, HGRN_DIAG // 2)
        p = _nt_dot((qc * jnp.exp(dref)).astype(jnp.bfloat16),
                    (kc * jnp.exp(-dref)).astype(jnp.bfloat16))
        intra.append(jnp.where(diag_mask, p, a).astype(jnp.bfloat16))

    o_intra, upds, qgs, decs = [], [], [], []
    for c, sl in enumerate(slices):
        gc = gcs[c]
        vc = v_ref[sl, :]
        g_last = gc[chunk - 1:chunk]
        k_dec = (k_in[sl] * jnp.exp(g_last - gc)).astype(jnp.bfloat16)
        o_intra.append(jnp.dot(intra[c], vc, preferred_element_type=jnp.float32))
        upds.append(lax.dot_general(vc, k_dec, (((0,), (0,)), ((), ())),
                                    preferred_element_type=jnp.float32))
        qgs.append((q_ref[sl, :] * jnp.exp(gc)).astype(jnp.bfloat16))
        decs.append(jnp.exp(g_last))

    st = st_ref[...]
    for c, sl in enumerate(slices):
        o = o_intra[c] + _nt_dot(qgs[c], st.astype(jnp.bfloat16))
        st = st * decs[c] + upds[c]

        ms = jnp.mean(o * o, axis=-1, keepdims=True)
        y = o * lax.rsqrt(ms + RMS_EPS) * nw
        gate = g_ref[sl, :]
        y = y * (gate * (1.0 / (1.0 + jnp.exp(-gate))))
        o_ref[sl, :] = y.astype(o_ref.dtype)
    st_ref[...] = st


def _hgrn(hqz, hg, hv, lb_logits, norm_w, layer, batch, seq, rows=512):
    rows = min(rows, seq)
    h = HGRN_HEADS
    d = HGRN_DIM
    depth = lb_logits.shape[0]
    head = lambda b, hh, r: (b, r, hh)
    return pl.pallas_call(
        functools.partial(_hgrn_kernel, layer=layer, rows=rows, chunk=HGRN_CHUNK),
        grid=(batch, h, seq // rows),
        in_specs=[pl.BlockSpec((None, rows, d), head),
                  pl.BlockSpec((None, rows, d), lambda b, hh, r: (b, r, h + hh)),
                  pl.BlockSpec((None, rows, d), head),
                  pl.BlockSpec((None, rows, d), head),
                  pl.BlockSpec((depth, d), lambda b, hh, r: (0, hh)),
                  pl.BlockSpec((None, 1, d), lambda b, hh, r: (layer, 0, hh))],
        out_specs=pl.BlockSpec((None, rows, d), head),
        out_shape=jax.ShapeDtypeStruct((batch, seq, h * d), jnp.bfloat16),
        scratch_shapes=[pltpu.VMEM((d, d), jnp.float32)],
        compiler_params=_params("arbitrary", "arbitrary", "arbitrary"),
        name="hgrn2",
    )(hqz, hqz, hg, hv, lb_logits, norm_w)


def _mix_out_kernel(a1_ref, a2_ref, w1_ref, w2_ref, x_ref, g_ref, b_ref, o_ref, ob_ref,
                    wb_ref, *, alpha):
    @pl.when(pl.program_id(0) == 0)
    def _():
        wb_ref[0] = w1_ref[...].astype(wb_ref.dtype)
        wb_ref[1] = w2_ref[...].astype(wb_ref.dtype)

    for sl in _row_slices(x_ref.shape[0]):
        acc = jnp.dot(a1_ref[sl, :], wb_ref[0], preferred_element_type=jnp.float32)
        acc = acc + jnp.dot(a2_ref[sl, :], wb_ref[1], preferred_element_type=jnp.float32)
        o = _layer_norm_rows(alpha * x_ref[sl, :] + acc, g_ref[...], b_ref[...])
        o_ref[sl, :] = o
        ob_ref[sl, :] = o.astype(ob_ref.dtype)


def _resident(block_shape, index_map):
    return pl.BlockSpec(block_shape, index_map, pipeline_mode=pl.Buffered(1))


def _mix_out(a1, a2, w, x, g, b, layer, alpha, tm=512):
    m, d = x.shape
    k = a1.shape[1]
    tm = min(tm, m)
    row = lambda i: (i, 0)
    par = lambda i: (layer, 0, 0)
    return pl.pallas_call(
        functools.partial(_mix_out_kernel, alpha=alpha),
        grid=(m // tm,),
        in_specs=[pl.BlockSpec((tm, k), row), pl.BlockSpec((tm, k), row),
                  _resident((None, k, d), par), _resident((None, k, d), lambda i: (layer, 1, 0)),
                  pl.BlockSpec((tm, d), row),
                  _resident((None, 1, d), par), _resident((None, 1, d), par)],
        out_specs=[pl.BlockSpec((tm, d), row), pl.BlockSpec((tm, d), row)],
        out_shape=[jax.ShapeDtypeStruct((m, d), jnp.float32),
                   jax.ShapeDtypeStruct((m, d), jnp.bfloat16)],
        scratch_shapes=[pltpu.VMEM((2, k, d), jnp.bfloat16)],
        compiler_params=_params("arbitrary"),
        name="mix_out_ln",
    )(a1, a2, w, w, x, g, b)


def _xattn_kernel(q_ref, k_ref, v_ref, wo_ref, x_ref, g_ref, b_ref, o_ref, ob_ref,
                  wb_ref, *, alpha, heads):
    @pl.when(pl.program_id(0) == 0)
    def _():
        wb_ref[...] = wo_ref[...].astype(wb_ref.dtype)

    d = q_ref.shape[1]
    dh = d // heads
    outs = []
    scores = [_nt_dot(q_ref[:, h * dh:(h + 1) * dh], k_ref[:, h * dh:(h + 1) * dh])
              for h in range(heads)]
    for h in range(heads):
        sl = slice(h * dh, (h + 1) * dh)
        s = scores[h]
        p = jnp.exp2(s - jnp.max(s, axis=-1, keepdims=True))
        p = p / jnp.sum(p, axis=-1, keepdims=True)
        outs.append(jnp.dot(p.astype(jnp.bfloat16), v_ref[:, sl],
                            preferred_element_type=jnp.float32).astype(jnp.bfloat16))
    att = jnp.concatenate(outs, axis=1)
    for sl in _row_slices(x_ref.shape[0]):
        acc = jnp.dot(att[sl], wb_ref[...], preferred_element_type=jnp.float32)
        o = _layer_norm_rows(alpha * x_ref[sl, :] + acc, g_ref[...], b_ref[...])
        o_ref[sl, :] = o
        ob_ref[sl, :] = o.astype(ob_ref.dtype)


def _xattn(q, k, v, wo, x, g, b, layer, alpha, batch, seq, tm=256):
    m, d = x.shape
    n_mem = k.shape[0] // batch
    tm = min(tm, seq)
    per_b = seq // tm
    row = lambda i: (i, 0)
    par = lambda i: (layer, 0, 0)
    mem_map = lambda i: (i // per_b, 0)
    return pl.pallas_call(
        functools.partial(_xattn_kernel, alpha=alpha, heads=XATTN_HEADS),
        grid=(m // tm,),
        in_specs=[pl.BlockSpec((tm, d), row),
                  pl.BlockSpec((n_mem, d), mem_map),
                  pl.BlockSpec((n_mem, d), mem_map),
                  _resident((None, d, d), par),
                  pl.BlockSpec((tm, d), row),
                  _resident((None, 1, d), par), _resident((None, 1, d), par)],
        out_specs=[pl.BlockSpec((tm, d), row), pl.BlockSpec((tm, d), row)],
        out_shape=[jax.ShapeDtypeStruct((m, d), jnp.float32),
                   jax.ShapeDtypeStruct((m, d), jnp.bfloat16)],
        scratch_shapes=[pltpu.VMEM((d, d), jnp.bfloat16)],
        compiler_params=_params("arbitrary"),
        name="xattn_out_ln",
    )(q, k, v, wo, x, g, b)


def _ffn_up_kernel(xb_ref, wa_ref, wb_ref, cwa_ref, cwb_ref, cba_ref, cbb_ref, o_ref,
                   wab_ref, halo_ref, *, tiles_per_seq, rs):
    i = pl.program_id(1)
    tm = xb_ref.shape[0]

    @pl.when(i == 0)
    def _():
        wab_ref[0] = wa_ref[...].astype(wab_ref.dtype)
        wab_ref[1] = wb_ref[...].astype(wab_ref.dtype)

    @pl.when(i % tiles_per_seq == 0)
    def _():
        halo_ref[...] = jnp.zeros_like(halo_ref)

    cws = (cwa_ref[...], cwb_ref[...])
    cbs = (cba_ref[...], cbb_ref[...])
    prev = [halo_ref[0], halo_ref[1]]
    for r in range(tm // rs):
        xr = xb_ref[r * rs:(r + 1) * rs, :]
        branches = []
        for t in range(2):
            h = jnp.dot(xr, wab_ref[t], preferred_element_type=jnp.float32)
            seam = jnp.concatenate([prev[t], h[:SUBLANES]], axis=0)
            h1 = jnp.concatenate([seam[SUBLANES - 1:2 * SUBLANES - 1],
                                  pltpu.roll(h, 1, axis=0)[SUBLANES:]], axis=0)
            h2 = jnp.concatenate([seam[SUBLANES - 2:2 * SUBLANES - 2],
                                  pltpu.roll(h, 2, axis=0)[SUBLANES:]], axis=0)
            prev[t] = h[rs - SUBLANES:rs]
            cw = cws[t]
            branches.append(cw[0:1] * h2 + cw[1:2] * h1 + cw[2:3] * h + cbs[t])
        a, bb = branches
        gated = (a * (1.0 / (1.0 + jnp.exp(-a)))) * bb
        o_ref[r * rs:(r + 1) * rs, :] = gated.astype(o_ref.dtype)
    halo_ref[0] = prev[0]
    halo_ref[1] = prev[1]


def _ffn_up(xb, w_up, conv_w, conv_b, layer, seq, tm=2048, tf=512, rs=256):
    m, d = xb.shape
    dff = w_up.shape[2] // 2
    tm = min(tm, seq)
    nj = dff // tf
    col_a = lambda j, i: (layer, 0, j)
    col_b = lambda j, i: (layer, 0, nj + j)
    return pl.pallas_call(
        functools.partial(_ffn_up_kernel, tiles_per_seq=seq // tm, rs=rs),
        grid=(nj, m // tm),
        in_specs=[pl.BlockSpec((tm, d), lambda j, i: (i, 0)),
                  pl.BlockSpec((None, d, tf), col_a), pl.BlockSpec((None, d, tf), col_b),
                  pl.BlockSpec((None, CONV_WIDTH, tf), col_a),
                  pl.BlockSpec((None, CONV_WIDTH, tf), col_b),
                  pl.BlockSpec((None, 1, tf), col_a), pl.BlockSpec((None, 1, tf), col_b)],
        out_specs=pl.BlockSpec((tm, tf), lambda j, i: (i, j)),
        out_shape=jax.ShapeDtypeStruct((m, dff), jnp.bfloat16),
        scratch_shapes=[pltpu.VMEM((2, d, tf), jnp.bfloat16),
                        pltpu.VMEM((2, SUBLANES, tf), jnp.float32)],
        compiler_params=_params("arbitrary", "arbitrary"),
        name="ffn_up_conv_gate",
    )(xb, w_up, w_up, conv_w, conv_w, conv_b, conv_b)


def _ffn_down_kernel(a_ref, w_ref, x_ref, g_ref, b_ref, o_ref, ob_ref, *, alpha):
    for sl in _row_slices(x_ref.shape[0]):
        acc = jnp.dot(a_ref[sl, :], w_ref[...], preferred_element_type=jnp.float32)
        o = _layer_norm_rows(alpha * x_ref[sl, :] + acc, g_ref[...], b_ref[...])
        o_ref[sl, :] = o
        ob_ref[sl, :] = o.astype(ob_ref.dtype)


def _ffn_down(a, w, x, g, b, layer, alpha, tm=256):
    m, d = x.shape
    f = a.shape[1]
    tm = min(tm, m)
    row = lambda i: (i, 0)
    par = lambda i: (layer, 0, 0)
    return pl.pallas_call(
        functools.partial(_ffn_down_kernel, alpha=alpha),
        grid=(m // tm,),
        in_specs=[pl.BlockSpec((tm, f), row),
                  _resident((f, d), lambda i: (0, 0)),
                  pl.BlockSpec((tm, d), row),
                  _resident((None, 1, d), par), _resident((None, 1, d), par)],
        out_specs=[pl.BlockSpec((tm, d), row), pl.BlockSpec((tm, d), row)],
        out_shape=[jax.ShapeDtypeStruct((m, d), jnp.float32),
                   jax.ShapeDtypeStruct((m, d), jnp.bfloat16)],
        compiler_params=_params("arbitrary"),
        name="ffn_down_ln",
    )(a, w, x, g, b)


def kernel(x, mem, w_in, fox_f_bias, hgrn_lb_logits, hgrn_norm_w, w_out, ln1_g, ln1_b,
           xq_w, xk_w, xv_w, xo_w, ln2_g, ln2_b, ffn_up, conv_w, conv_b, ffn_down,
           ln3_g, ln3_b):
    batch, seq, d = x.shape
    depth = w_in.shape[0]
    alpha = (2 * depth) ** 0.25
    bf = jnp.bfloat16
    fw = FOX_HEADS * FOX_HEAD_DIM
    hw = HGRN_HEADS * HGRN_DIM
    m = batch * seq
    n_mem = mem.shape[1]

    o_fq, o_fk, o_fv, o_ff = 0, fw, 2 * fw, 3 * fw
    o_hq = o_ff + FOX_HEADS
    o_hf, o_hi, o_hg = o_hq + hw, o_hq + 2 * hw, o_hq + 3 * hw

    tn = 1024
    ones = jnp.ones((1, tn), jnp.float32)
    qk_scale = jnp.concatenate(
        [jnp.full((1, fw), FOX_HEAD_DIM ** -0.5 * LOG2E, jnp.float32), ones], axis=1)
    hqz_scale = jnp.concatenate([jnp.full((1, hw), HGRN_DIM ** -0.5, jnp.float32), ones], axis=1)
    xq_scale = jnp.full((1, d), (d // XATTN_HEADS) ** -0.5 * LOG2E, jnp.float32)
    d_ones = jnp.ones((1, d), jnp.float32)

    xf = x.reshape(m, d)
    xb = xf.astype(bf)
    memb = mem.reshape(batch * n_mem, d).astype(bf)
    norm_w = hgrn_norm_w.reshape(depth, 1, hw)
    as_par = lambda p: p.reshape(depth, 1, -1)
    ln1 = (as_par(ln1_g), as_par(ln1_b))
    ln2 = (as_par(ln2_g), as_par(ln2_b))
    ln3 = (as_par(ln3_g), as_par(ln3_b))
    conv_b3 = as_par(conv_b)

    w_in_t = jnp.swapaxes(w_in, 1, 2)

    for l in range(depth):
        f_bias = jnp.zeros((1, FGATE_LANES), jnp.float32).at[0, :FOX_HEADS].set(fox_f_bias[l])

        qk = _matmul_nt(xb, w_in_t, l, o_fq, 2, qk_scale, bf, 1024, tn, "in_proj_qk")
        vt = _matmul_nt(xb, w_in_t, l, o_fv, 1, ones, bf, 1024, tn, "in_proj_vt",
                        transpose_out=True)
        hqz = _matmul_nt(xb, w_in_t, l, o_hq, 2, hqz_scale, jnp.float32, 1024, tn, "in_proj_hqz")
        hv = _matmul_nt(xb, w_in_t, l, o_hi, 1, ones, bf, 1024, tn, "in_proj_hv")
        hg = _matmul_nt(xb, w_in_t, l, o_hg, 1, ones, jnp.float32, 1024, tn, "in_proj_hg")
        ka, qa = _fox_bias_operands(xb, w_in_t, l, o_ff, f_bias, batch, seq)
        fox_out = _fox_attention(qk.reshape(batch, seq, 2 * fw), qa.reshape(batch, seq, fw),
                                 ka.reshape(batch, seq, fw), vt, batch, seq)
        h_out = _hgrn(hqz.reshape(batch, seq, 2 * hw), hg.reshape(batch, seq, hw),
                      hv.reshape(batch, seq, hw), hgrn_lb_logits, norm_w, l, batch, seq)
        xf, xb = _mix_out(fox_out.reshape(m, fw), h_out.reshape(m, hw), w_out, xf, *ln1, l, alpha)

        q = _matmul(xb, xq_w, l, xq_scale, bf, 1024, tn, "xattn_q")
        k = _matmul(memb, xk_w, l, d_ones, bf, 1024, tn, "xattn_k")
        v = _matmul(memb, xv_w, l, d_ones, bf, 1024, tn, "xattn_v")
        xf, xb = _xattn(q, k, v, xo_w, xf, *ln2, l, alpha, batch, seq)

        gated = _ffn_up(xb, ffn_up, conv_w, conv_b3, l, seq)
        xf, xb = _ffn_down(gated, _cast_layer(ffn_down, l), xf, *ln3, l, alpha)
    return xf.reshape(batch, seq, d)
```

```python
import functools

import jax
import jax.numpy as jnp
import numpy as np
from jax import lax
from jax.experimental import pallas as pl
from jax.experimental.pallas import tpu as pltpu

FOX_HEADS = 8
FOX_HEAD_DIM = 128
HGRN_HEADS = 8
HGRN_DIM = 128
XATTN_HEADS = 4
CONV_WIDTH = 3
LN_EPS = 1e-5
RMS_EPS = 1e-6
MASK_VALUE = -1e30
MIN_FORGET = 1e-6
LOG2E = 1.4426950408889634

V7X_VMEM_LIMIT_BYTES = 56 * 1024 * 1024
SUBLANES = 8
BF16_SUBLANES = 16
HGRN_CHUNK = 128
HGRN_DIAG = 8
LN_SUB_ROWS = 128
FGATE_LANES = 128
N_PIECES = 3


def _params(*sem):
    return pltpu.CompilerParams(dimension_semantics=sem,
                                vmem_limit_bytes=V7X_VMEM_LIMIT_BYTES)


def _nt_dot(a, b):
    return lax.dot_general(a, b, (((1,), (1,)), ((), ())),
                           preferred_element_type=jnp.float32)


def _row_slices(rows, sub=LN_SUB_ROWS):
    sub = min(sub, rows)
    return [slice(r, r + sub) for r in range(0, rows, sub)]


def _layer_norm_rows(y, g, b):
    mu = jnp.mean(y, axis=-1, keepdims=True)
    d = y - mu
    var = jnp.mean(d * d, axis=-1, keepdims=True)
    return d * lax.rsqrt(var + LN_EPS) * g + b


def _mm_kernel(a_ref, w_ref, s_ref, o_ref, wb_ref):
    @pl.when(pl.program_id(1) == 0)
    def _():
        wb_ref[...] = w_ref[...].astype(wb_ref.dtype)

    acc = jnp.dot(a_ref[...], wb_ref[...], preferred_element_type=jnp.float32)
    o_ref[...] = (acc * s_ref[...]).astype(o_ref.dtype)


def _matmul(a, w, layer, col_scale, out_dtype, tm, tn, name):
    m, k = a.shape
    n = w.shape[2]
    tm = min(tm, m)
    return pl.pallas_call(
        _mm_kernel,
        grid=(n // tn, m // tm),
        in_specs=[pl.BlockSpec((tm, k), lambda j, i: (i, 0)),
                  pl.BlockSpec((None, k, tn), lambda j, i: (layer, 0, j)),
                  pl.BlockSpec((1, tn), lambda j, i: (0, j))],
        out_specs=pl.BlockSpec((tm, tn), lambda j, i: (i, j)),
        out_shape=jax.ShapeDtypeStruct((m, n), out_dtype),
        scratch_shapes=[pltpu.VMEM((k, tn), jnp.bfloat16)],
        compiler_params=_params("arbitrary", "arbitrary"),
        name=name,
    )(a, w, col_scale)


def _mm_nt_kernel(a_ref, w_ref, s_ref, o_ref, wb_ref, *, transpose_out):
    @pl.when(pl.program_id(1) == 0)
    def _():
        wb_ref[...] = w_ref[...].astype(wb_ref.dtype)

    if transpose_out:
        o_ref[...] = _nt_dot(wb_ref[...], a_ref[...]).astype(o_ref.dtype)
    else:
        o_ref[...] = (_nt_dot(a_ref[...], wb_ref[...]) * s_ref[...]).astype(o_ref.dtype)


def _matmul_nt(a, wt, layer, row0, n_blocks, col_scale, out_dtype, tm, tn, name,
               transpose_out=False):
    m, k = a.shape
    tm = min(tm, m)
    assert row0 % SUBLANES == 0
    n = n_blocks * tn
    if transpose_out:
        out_spec = pl.BlockSpec((tn, tm), lambda j, i: (j, i))
        out_shape = jax.ShapeDtypeStruct((n, m), out_dtype)
    else:
        out_spec = pl.BlockSpec((tm, tn), lambda j, i: (i, j))
        out_shape = jax.ShapeDtypeStruct((m, n), out_dtype)
    return pl.pallas_call(
        functools.partial(_mm_nt_kernel, transpose_out=transpose_out),
        grid=(n_blocks, m // tm),
        in_specs=[pl.BlockSpec((tm, k), lambda j, i: (i, 0)),
                  pl.BlockSpec((None, pl.Element(tn), pl.Element(k)),
                               lambda j, i: (layer, pl.multiple_of(row0 + j * tn, SUBLANES), 0)),
                  pl.BlockSpec((1, tn), lambda j, i: (0, j))],
        out_specs=out_spec,
        out_shape=out_shape,
        scratch_shapes=[pltpu.VMEM((tn, k), jnp.bfloat16)],
        compiler_params=_params("arbitrary", "arbitrary"),
        name=name,
    )(a, wt, col_scale)


def _cast_kernel(w_ref, o_ref):
    o_ref[...] = w_ref[...].astype(o_ref.dtype)


def _cast_layer(w, layer, rows=512):
    _, k, n = w.shape
    return pl.pallas_call(
        _cast_kernel,
        grid=(k // rows,),
        in_specs=[pl.BlockSpec((None, rows, n), lambda r: (layer, r, 0))],
        out_specs=pl.BlockSpec((rows, n), lambda r: (r, 0)),
        out_shape=jax.ShapeDtypeStruct((k, n), jnp.bfloat16),
        compiler_params=_params("arbitrary"),
        name="cast_layer_bf16",
    )(w)


def _fgate_kernel(x_ref, wt_ref, b_ref, selk_ref, selq_ref, onek_ref, oneq_ref,
                  ka_ref, qa_ref, carry_ref, wf_ref, *, ts, heads):
    @pl.when(pl.program_id(1) == 0)
    def _():
        carry_ref[...] = jnp.zeros_like(carry_ref)
        w = wt_ref[...]
        row = lax.broadcasted_iota(jnp.int32, w.shape, 0)
        wf_ref[...] = jnp.where(row < heads, w, 0.0).astype(wf_ref.dtype)

    half = ts // 2
    halves = (slice(0, half), slice(half, ts))
    z = jnp.concatenate([_nt_dot(x_ref[sl, :], wf_ref[...]) for sl in halves], axis=0) + b_ref[...]
    lf = jnp.minimum(z, 0.0) - jnp.log(1.0 + jnp.exp(-jnp.abs(z)))
    row = lax.broadcasted_iota(jnp.int32, lf.shape, 0)
    k = 1
    while k < ts:
        lf = lf + jnp.where(row >= k, pltpu.roll(lf, k, axis=0), 0.0)
        k *= 2
    c = lf + carry_ref[0:1, :]
    carry_ref[...] = jnp.broadcast_to(c[ts - 1:ts, :], carry_ref.shape)
    c2 = c * LOG2E
    hi = c2.astype(jnp.bfloat16)
    r1 = c2 - hi.astype(jnp.float32)
    mid = r1.astype(jnp.bfloat16)
    lo = (r1 - mid.astype(jnp.float32)).astype(jnp.bfloat16)
    pieces = jnp.concatenate([hi, mid, lo], axis=1)
    for sl in halves:
        ka = jnp.dot(pieces[sl], selk_ref[...], preferred_element_type=jnp.float32) + onek_ref[...]
        qa = jnp.dot(pieces[sl], selq_ref[...], preferred_element_type=jnp.float32) + oneq_ref[...]
        ka_ref[sl, :] = ka.astype(ka_ref.dtype)
        qa_ref[sl, :] = qa.astype(qa_ref.dtype)


def _fox_bias_operands(xb, wt, layer, row0, bias, batch, seq, ts=512):
    m, d = xb.shape
    ts = min(ts, seq)
    nt = seq // ts
    lanes = FGATE_LANES
    heads = FOX_HEADS
    width = heads * FOX_HEAD_DIM
    piece = np.arange(N_PIECES)
    head = np.arange(heads)
    rows = (piece[None, :] * lanes + head[:, None]).reshape(-1)
    cols_k = (head[:, None] * FOX_HEAD_DIM + piece[None, :]).reshape(-1)
    selk = np.zeros((N_PIECES * lanes, width), np.float32)
    selq = np.zeros((N_PIECES * lanes, width), np.float32)
    onek = np.zeros((1, width), np.float32)
    oneq = np.zeros((1, width), np.float32)
    selk[rows, cols_k] = -1.0
    selq[rows, cols_k + N_PIECES] = 1.0
    onek[0, cols_k + N_PIECES] = 1.0
    oneq[0, cols_k] = 1.0
    fixed = lambda b, j: (0, 0)
    row = lambda b, j: (b * nt + j, 0)
    return pl.pallas_call(
        functools.partial(_fgate_kernel, ts=ts, heads=heads),
        grid=(batch, nt),
        in_specs=[pl.BlockSpec((ts, d), row),
                  pl.BlockSpec((None, pl.Element(lanes), pl.Element(d)),
                               lambda b, j: (layer, row0, 0)),
                  pl.BlockSpec((1, lanes), fixed),
                  pl.BlockSpec((N_PIECES * lanes, width), fixed),
                  pl.BlockSpec((N_PIECES * lanes, width), fixed),
                  pl.BlockSpec((1, width), fixed),
                  pl.BlockSpec((1, width), fixed)],
        out_specs=[pl.BlockSpec((ts, width), row), pl.BlockSpec((ts, width), row)],
        out_shape=[jax.ShapeDtypeStruct((m, width), jnp.bfloat16),
                   jax.ShapeDtypeStruct((m, width), jnp.bfloat16)],
        scratch_shapes=[pltpu.VMEM((SUBLANES, lanes), jnp.float32),
                        pltpu.VMEM((lanes, d), jnp.bfloat16)],
        compiler_params=_params("arbitrary", "arbitrary"),
        name="fox_bias_operands",
    )(xb, wt, bias, jnp.asarray(selk, jnp.bfloat16), jnp.asarray(selq, jnp.bfloat16),
      jnp.asarray(onek), jnp.asarray(oneq))


def _fox_kernel(q_ref, qa_ref, k_ref, ka_ref, vt_ref, o_ref, *score_refs, tq, tk, sub):
    sc0_ref = score_refs[:len(score_refs) // 2]
    sc1_ref = score_refs[len(score_refs) // 2:]
    i = pl.program_id(2)
    dh = q_ref.shape[1]
    nsub = tq // sub
    q_ext = [jnp.concatenate([q_ref[s * sub:(s + 1) * sub, :], qa_ref[s * sub:(s + 1) * sub, :]],
                             axis=1) for s in range(nsub)]

    def kv_block(k0, n):
        k_ext = jnp.concatenate([k_ref[pl.ds(k0, n), :], ka_ref[pl.ds(k0, n), :]], axis=1)
        return k_ext, vt_ref[:, pl.ds(k0, n)]

    def probs(carry, s):
        m, _ = carry
        m_new = jnp.maximum(m, jnp.max(s, axis=0, keepdims=True))
        p = jnp.exp2(s - m_new).astype(jnp.bfloat16)
        return m_new, jnp.exp2(m - m_new), p

    def accumulate(carry, soft, vt):
        m_new, alpha, p = soft
        vt_ones = jnp.concatenate([vt, jnp.ones((BF16_SUBLANES, vt.shape[1]), vt.dtype)], axis=0)
        acc = alpha * carry[1] + jnp.dot(vt_ones, p, preferred_element_type=jnp.float32)
        return m_new, acc

    def update(carry, s, vt):
        return accumulate(carry, probs(carry, s), vt)

    def put_scores(j, sc_ref):
        k_ext, _ = kv_block(pl.multiple_of(j * tk, tk), tk)
        for s in range(nsub):
            sc_ref[s][...] = _nt_dot(k_ext, q_ext[s])

    def soft_block(sc_ref, carries):
        return [probs(c, sc_ref[s][...]) for s, c in enumerate(carries)]

    def pv_block(j, softs, carries):
        _, vt = kv_block(pl.multiple_of(j * tk, tk), tk)
        return tuple(accumulate(c, soft, vt) for c, soft in zip(carries, softs))

    def causal(sc, k_lo, q_lo):
        key = lax.broadcasted_iota(jnp.int32, sc.shape, 0) + k_lo
        qry = lax.broadcasted_iota(jnp.int32, sc.shape, 1) + q_lo
        return jnp.where(key <= qry, sc, MASK_VALUE)

    def body(jj, carries):
        j = 2 * jj
        put_scores(j + 1, sc1_ref)
        soft = soft_block(sc0_ref, carries)
        put_scores(j + 2, sc0_ref)
        carries = pv_block(j, soft, carries)
        soft = soft_block(sc1_ref, carries)
        return pv_block(j + 1, soft, carries)

    init = tuple((jnp.full((1, sub), MASK_VALUE, jnp.float32),
                  jnp.zeros((dh + BF16_SUBLANES, sub), jnp.float32)) for _ in range(nsub))
    put_scores(0, sc0_ref)
    carries = list(lax.fori_loop(0, i * (tq // tk // 2), body, init))

    base = i * tq
    later = []
    for kb in range(1, tq // tk):
        for s in range(nsub):
            q_lo, q_hi = s * sub, (s + 1) * sub
            k_lo = kb * tk
            k_hi = min(k_lo + tk, q_hi)
            if k_hi <= k_lo:
                continue
            k_ext, vt = kv_block(pl.multiple_of(base + k_lo, sub), k_hi - k_lo)
            sc = _nt_dot(k_ext, q_ext[s])
            later.append((s, causal(sc, k_lo, q_lo) if k_hi - 1 > q_lo else sc, vt))
    _, vt = kv_block(pl.multiple_of(base, tk), tk)
    for s in range(nsub):
        sc = sc0_ref[s][...] if tk - 1 <= s * sub else causal(sc0_ref[s][...], 0, s * sub)
        carries[s] = update(carries[s], sc, vt)
    for s, sc, vt in later:
        carries[s] = update(carries[s], sc, vt)
    out = jnp.concatenate([acc[:dh] / acc[dh:dh + 1] for _, acc in carries], axis=1)
    o_ref[...] = out.T.astype(o_ref.dtype)


def _fox_attention(proj, qa, ka, vt, batch, seq, tq=1024, tk=512, sub=256):
    tq = min(tq, seq)
    assert tq % (2 * tk) == 0 and tk % sub == 0
    dh = FOX_HEAD_DIM
    h = FOX_HEADS
    score_bufs = [pltpu.VMEM((tk, sub), jnp.float32)] * (2 * (tq // sub))
    q_map = lambda b, hh, i: (b, i, hh)
    k_map = lambda b, hh, i: (b, 0, hh)
    return pl.pallas_call(
        functools.partial(_fox_kernel, tq=tq, tk=tk, sub=sub),
        grid=(batch, h, seq // tq),
        in_specs=[pl.BlockSpec((None, tq, dh), q_map),
                  pl.BlockSpec((None, tq, dh), q_map),
                  pl.BlockSpec((None, seq, dh), lambda b, hh, i: (b, 0, h + hh)),
                  pl.BlockSpec((None, seq, dh), k_map),
                  pl.BlockSpec((dh, seq), lambda b, hh, i: (hh, b))],
        out_specs=pl.BlockSpec((None, tq, dh), q_map),
        out_shape=jax.ShapeDtypeStruct((batch, seq, h * dh), jnp.bfloat16),
        scratch_shapes=score_bufs,
        compiler_params=_params("arbitrary", "arbitrary", "arbitrary"),
        name="fox_attention",
    )(proj, qa, proj, ka, vt)


def _group_ref(g, group, ref_row):
    c, d = g.shape
    g3 = g.reshape(c // group, group, d)
    return jnp.broadcast_to(g3[:, ref_row:ref_row + 1, :], g3.shape).reshape(c, d)


def _hgrn_kernel(q_ref, z_ref, g_ref, v_ref, lbl_ref, nw_ref, o_ref, st_ref,
                 *, layer, rows, chunk):
    @pl.when(pl.program_id(2) == 0)
    def _():
        st_ref[...] = jnp.zeros_like(st_ref)

    lg = lbl_ref[...]
    e = jnp.exp(lg - jnp.max(lg, axis=0, keepdims=True))
    soft = e / jnp.sum(e, axis=0, keepdims=True)
    cs = soft[0:1]
    for idx in range(1, layer + 1):
        cs = cs + soft[idx:idx + 1]
    lb = cs - soft[0:1]

    z = z_ref[...]
    ez = jnp.exp(-jnp.abs(z))
    r = 1.0 / (1.0 + ez)
    pos = z >= 0.0
    sig = jnp.where(pos, r, ez * r)
    nsig = jnp.where(pos, ez * r, r)
    f_gate = lb + (1.0 - lb) * sig
    logf = jnp.log2(jnp.maximum(f_gate, MIN_FORGET))
    k_in = (1.0 - lb) * nsig

    rt = lax.broadcasted_iota(jnp.int32, (chunk, chunk), 0)
    cc = lax.broadcasted_iota(jnp.int32, (chunk, chunk), 1)
    diag_mask = ((rt // HGRN_DIAG) == (cc // HGRN_DIAG)) & (cc <= rt)
    tri_r = lax.broadcasted_iota(jnp.int32, (chunk, N_PIECES * chunk), 0)
    tri_c = lax.broadcasted_iota(jnp.int32, (chunk, N_PIECES * chunk), 1) & (chunk - 1)
    tri = jnp.where(tri_c <= tri_r, 1.0, 0.0).astype(jnp.bfloat16)
    zeros = {}
    span = chunk // 2
    while span >= HGRN_DIAG:
        zeros[span] = jnp.zeros((chunk // (2 * span), span, HGRN_DIM), jnp.float32)
        span //= 2

    nw = nw_ref[...]
    n_chunks = rows // chunk
    slices = [slice(c * chunk, (c + 1) * chunk) for c in range(n_chunks)]

    hi = logf.astype(jnp.bfloat16)
    r1 = logf - hi.astype(jnp.float32)
    mid = r1.astype(jnp.bfloat16)
    lo = (r1 - mid.astype(jnp.float32)).astype(jnp.bfloat16)
    gcs = [jnp.dot(tri, jnp.concatenate([hi[sl], mid[sl], lo[sl]], axis=0),
                   preferred_element_type=jnp.float32) for sl in slices]

    intra = []
    for c, sl in enumerate(slices):
        gc = gcs[c]
        qc = q_ref[sl, :]
        kc = k_in[sl]

        a = None
        span = chunk // 2
        while span >= HGRN_DIAG:
            group = 2 * span
            ng = chunk // group
            g3 = gc.reshape(ng, group, HGRN_DIM)
            gref = g3[:, span:span + 1, :]
            q_up = qc.reshape(ng, group, HGRN_DIM)[:, span:, :] * jnp.exp2(g3[:, span:, :] - gref)
            k_lo = kc.reshape(ng, group, HGRN_DIM)[:, :span, :] * jnp.exp2(gref - g3[:, :span, :])
            ql = jnp.concatenate([zeros[span], q_up], axis=1).reshape(chunk, HGRN_DIM)
            kl = jnp.concatenate([k_lo, zeros[span]], axis=1).reshape(chunk, HGRN_DIM)
            p = _nt_dot(ql.astype(jnp.bfloat16), kl.astype(jnp.bfloat16))
            a = p if a is None else jnp.where((rt // group) == (cc // group), p, a)
            span //= 2
        dref = gc - _group_ref(gc, HGRN_DIAG, HGRN_DIAG // 2)
        p = _nt_dot((qc * jnp.exp2(dref)).astype(jnp.bfloat16),
                    (kc * jnp.exp2(-dref)).astype(jnp.bfloat16))
        intra.append(jnp.where(diag_mask, p, a).astype(jnp.bfloat16))

    o_intra, upds, qgs, decs = [], [], [], []
    for c, sl in enumerate(slices):
        gc = gcs[c]
        vc = v_ref[sl, :]
        g_last = gc[chunk - 1:chunk]
        k_dec = (k_in[sl] * jnp.exp2(g_last - gc)).astype(jnp.bfloat16)
        o_intra.append(jnp.dot(intra[c], vc, preferred_element_type=jnp.float32))
        upds.append(lax.dot_general(vc, k_dec, (((0,), (0,)), ((), ())),
                                    preferred_element_type=jnp.float32))
        qgs.append((q_ref[sl, :] * jnp.exp2(gc)).astype(jnp.bfloat16))
        decs.append(jnp.exp2(g_last))

    st = st_ref[...]
    for c, sl in enumerate(slices):
        o = o_intra[c] + _nt_dot(qgs[c], st.astype(jnp.bfloat16))
        st = st * decs[c] + upds[c]

        ms = jnp.mean(o * o, axis=-1, keepdims=True)
        y = o * lax.rsqrt(ms + RMS_EPS) * nw
        gate = g_ref[sl, :]
        y = y * (gate * (1.0 / (1.0 + jnp.exp(-gate))))
        o_ref[sl, :] = y.astype(o_ref.dtype)
    st_ref[...] = st


def _hgrn(hqz, hg, hv, lb_logits, norm_w, layer, batch, seq, rows=512):
    rows = min(rows, seq)
    h = HGRN_HEADS
    d = HGRN_DIM
    depth = lb_logits.shape[0]
    head = lambda b, hh, r: (b, r, hh)
    return pl.pallas_call(
        functools.partial(_hgrn_kernel, layer=layer, rows=rows, chunk=HGRN_CHUNK),
        grid=(batch, h, seq // rows),
        in_specs=[pl.BlockSpec((None, rows, d), head),
                  pl.BlockSpec((None, rows, d), lambda b, hh, r: (b, r, h + hh)),
                  pl.BlockSpec((None, rows, d), head),
                  pl.BlockSpec((None, rows, d), head),
                  pl.BlockSpec((depth, d), lambda b, hh, r: (0, hh)),
                  pl.BlockSpec((None, 1, d), lambda b, hh, r: (layer, 0, hh))],
        out_specs=pl.BlockSpec((None, rows, d), head),
        out_shape=jax.ShapeDtypeStruct((batch, seq, h * d), jnp.bfloat16),
        scratch_shapes=[pltpu.VMEM((d, d), jnp.float32)],
        compiler_params=_params("arbitrary", "arbitrary", "arbitrary"),
        name="hgrn2",
    )(hqz, hqz, hg, hv, lb_logits, norm_w)


def _mix_out_kernel(a1_ref, a2_ref, w1_ref, w2_ref, x_ref, g_ref, b_ref, o_ref, ob_ref,
                    wb_ref, *, alpha):
    @pl.when(pl.program_id(0) == 0)
    def _():
        wb_ref[0] = w1_ref[...].astype(wb_ref.dtype)
        wb_ref[1] = w2_ref[...].astype(wb_ref.dtype)

    for sl in _row_slices(x_ref.shape[0]):
        acc = jnp.dot(a1_ref[sl, :], wb_ref[0], preferred_element_type=jnp.float32)
        acc = acc + jnp.dot(a2_ref[sl, :], wb_ref[1], preferred_element_type=jnp.float32)
        o = _layer_norm_rows(alpha * x_ref[sl, :] + acc, g_ref[...], b_ref[...])
        o_ref[sl, :] = o
        ob_ref[sl, :] = o.astype(ob_ref.dtype)


def _resident(block_shape, index_map):
    return pl.BlockSpec(block_shape, index_map, pipeline_mode=pl.Buffered(1))


def _mix_out(a1, a2, w, x, g, b, layer, alpha, tm=512):
    m, d = x.shape
    k = a1.shape[1]
    tm = min(tm, m)
    row = lambda i: (i, 0)
    par = lambda i: (layer, 0, 0)
    return pl.pallas_call(
        functools.partial(_mix_out_kernel, alpha=alpha),
        grid=(m // tm,),
        in_specs=[pl.BlockSpec((tm, k), row), pl.BlockSpec((tm, k), row),
                  _resident((None, k, d), par), _resident((None, k, d), lambda i: (layer, 1, 0)),
                  pl.BlockSpec((tm, d), row),
                  _resident((None, 1, d), par), _resident((None, 1, d), par)],
        out_specs=[pl.BlockSpec((tm, d), row), pl.BlockSpec((tm, d), row)],
        out_shape=[jax.ShapeDtypeStruct((m, d), jnp.float32),
                   jax.ShapeDtypeStruct((m, d), jnp.bfloat16)],
        scratch_shapes=[pltpu.VMEM((2, k, d), jnp.bfloat16)],
        compiler_params=_params("arbitrary"),
        name="mix_out_ln",
    )(a1, a2, w, w, x, g, b)


def _xattn_kernel(q_ref, k_ref, v_ref, wo_ref, x_ref, g_ref, b_ref, o_ref, ob_ref,
                  wb_ref, *, alpha, heads):
    @pl.when(pl.program_id(0) == 0)
    def _():
        wb_ref[...] = wo_ref[...].astype(wb_ref.dtype)

    d = q_ref.shape[1]
    dh = d // heads
    outs = []
    scores = [_nt_dot(q_ref[:, h * dh:(h + 1) * dh], k_ref[:, h * dh:(h + 1) * dh])
              for h in range(heads)]
    for h in range(heads):
        sl = slice(h * dh, (h + 1) * dh)
        s = scores[h]
        p = jnp.exp2(s - jnp.max(s, axis=-1, keepdims=True))
        p = p / jnp.sum(p, axis=-1, keepdims=True)
        outs.append(jnp.dot(p.astype(jnp.bfloat16), v_ref[:, sl],
                            preferred_element_type=jnp.float32).astype(jnp.bfloat16))
    att = jnp.concatenate(outs, axis=1)
    for sl in _row_slices(x_ref.shape[0]):
        acc = jnp.dot(att[sl], wb_ref[...], preferred_element_type=jnp.float32)
        o = _layer_norm_rows(alpha * x_ref[sl, :] + acc, g_ref[...], b_ref[...])
        o_ref[sl, :] = o
        ob_ref[sl, :] = o.astype(ob_ref.dtype)


def _xattn(q, k, v, wo, x, g, b, layer, alpha, batch, seq, tm=256):
    m, d = x.shape
    n_mem = k.shape[0] // batch
    tm = min(tm, seq)
    per_b = seq // tm
    row = lambda i: (i, 0)
    par = lambda i: (layer, 0, 0)
    mem_map = lambda i: (i // per_b, 0)
    return pl.pallas_call(
        functools.partial(_xattn_kernel, alpha=alpha, heads=XATTN_HEADS),
        grid=(m // tm,),
        in_specs=[pl.BlockSpec((tm, d), row),
                  pl.BlockSpec((n_mem, d), mem_map),
                  pl.BlockSpec((n_mem, d), mem_map),
                  _resident((None, d, d), par),
                  pl.BlockSpec((tm, d), row),
                  _resident((None, 1, d), par), _resident((None, 1, d), par)],
        out_specs=[pl.BlockSpec((tm, d), row), pl.BlockSpec((tm, d), row)],
        out_shape=[jax.ShapeDtypeStruct((m, d), jnp.float32),
                   jax.ShapeDtypeStruct((m, d), jnp.bfloat16)],
        scratch_shapes=[pltpu.VMEM((d, d), jnp.bfloat16)],
        compiler_params=_params("arbitrary"),
        name="xattn_out_ln",
    )(q, k, v, wo, x, g, b)


def _ffn_up_kernel(xb_ref, wa_ref, wb_ref, cwa_ref, cwb_ref, cba_ref, cbb_ref, o_ref,
                   wab_ref, halo_ref, *, tiles_per_seq, rs):
    i = pl.program_id(1)
    tm = xb_ref.shape[0]

    @pl.when(i == 0)
    def _():
        wab_ref[0] = wa_ref[...].astype(wab_ref.dtype)
        wab_ref[1] = wb_ref[...].astype(wab_ref.dtype)

    @pl.when(i % tiles_per_seq == 0)
    def _():
        halo_ref[...] = jnp.zeros_like(halo_ref)

    cws = (cwa_ref[...], cwb_ref[...])
    cbs = (cba_ref[...], cbb_ref[...])
    prev = [halo_ref[0], halo_ref[1]]
    for r in range(tm // rs):
        xr = xb_ref[r * rs:(r + 1) * rs, :]
        branches = []
        for t in range(2):
            h = jnp.dot(xr, wab_ref[t], preferred_element_type=jnp.float32)
            seam = jnp.concatenate([prev[t], h[:SUBLANES]], axis=0)
            h1 = jnp.concatenate([seam[SUBLANES - 1:2 * SUBLANES - 1],
                                  pltpu.roll(h, 1, axis=0)[SUBLANES:]], axis=0)
            h2 = jnp.concatenate([seam[SUBLANES - 2:2 * SUBLANES - 2],
                                  pltpu.roll(h, 2, axis=0)[SUBLANES:]], axis=0)
            prev[t] = h[rs - SUBLANES:rs]
            cw = cws[t]
            branches.append(cw[0:1] * h2 + cw[1:2] * h1 + cw[2:3] * h + cbs[t])
        a, bb = branches
        gated = (a * (1.0 / (1.0 + jnp.exp(-a)))) * bb
        o_ref[r * rs:(r + 1) * rs, :] = gated.astype(o_ref.dtype)
    halo_ref[0] = prev[0]
    halo_ref[1] = prev[1]


def _ffn_up(xb, w_up, conv_w, conv_b, layer, seq, tm=2048, tf=512, rs=128):
    m, d = xb.shape
    dff = w_up.shape[2] // 2
    tm = min(tm, seq)
    nj = dff // tf
    col_a = lambda j, i: (layer, 0, j)
    col_b = lambda j, i: (layer, 0, nj + j)
    return pl.pallas_call(
        functools.partial(_ffn_up_kernel, tiles_per_seq=seq // tm, rs=rs),
        grid=(nj, m // tm),
        in_specs=[pl.BlockSpec((tm, d), lambda j, i: (i, 0)),
                  pl.BlockSpec((None, d, tf), col_a), pl.BlockSpec((None, d, tf), col_b),
                  pl.BlockSpec((None, CONV_WIDTH, tf), col_a),
                  pl.BlockSpec((None, CONV_WIDTH, tf), col_b),
                  pl.BlockSpec((None, 1, tf), col_a), pl.BlockSpec((None, 1, tf), col_b)],
        out_specs=pl.BlockSpec((tm, tf), lambda j, i: (i, j)),
        out_shape=jax.ShapeDtypeStruct((m, dff), jnp.bfloat16),
        scratch_shapes=[pltpu.VMEM((2, d, tf), jnp.bfloat16),
                        pltpu.VMEM((2, SUBLANES, tf), jnp.float32)],
        compiler_params=_params("arbitrary", "arbitrary"),
        name="ffn_up_conv_gate",
    )(xb, w_up, w_up, conv_w, conv_w, conv_b, conv_b)


def _ffn_down_kernel(a_ref, w_ref, x_ref, g_ref, b_ref, o_ref, ob_ref, *, alpha):
    for sl in _row_slices(x_ref.shape[0]):
        acc = jnp.dot(a_ref[sl, :], w_ref[...], preferred_element_type=jnp.float32)
        o = _layer_norm_rows(alpha * x_ref[sl, :] + acc, g_ref[...], b_ref[...])
        o_ref[sl, :] = o
        ob_ref[sl, :] = o.astype(ob_ref.dtype)


def _ffn_down(a, w, x, g, b, layer, alpha, tm=256):
    m, d = x.shape
    f = a.shape[1]
    tm = min(tm, m)
    row = lambda i: (i, 0)
    par = lambda i: (layer, 0, 0)
    return pl.pallas_call(
        functools.partial(_ffn_down_kernel, alpha=alpha),
        grid=(m // tm,),
        in_specs=[pl.BlockSpec((tm, f), row),
                  _resident((f, d), lambda i: (0, 0)),
                  pl.BlockSpec((tm, d), row),
                  _resident((None, 1, d), par), _resident((None, 1, d), par)],
        out_specs=[pl.BlockSpec((tm, d), row), pl.BlockSpec((tm, d), row)],
        out_shape=[jax.ShapeDtypeStruct((m, d), jnp.float32),
                   jax.ShapeDtypeStruct((m, d), jnp.bfloat16)],
        compiler_params=_params("arbitrary"),
        name="ffn_down_ln",
    )(a, w, x, g, b)


def kernel(x, mem, w_in, fox_f_bias, hgrn_lb_logits, hgrn_norm_w, w_out, ln1_g, ln1_b,
           xq_w, xk_w, xv_w, xo_w, ln2_g, ln2_b, ffn_up, conv_w, conv_b, ffn_down,
           ln3_g, ln3_b):
    batch, seq, d = x.shape
    depth = w_in.shape[0]
    alpha = (2 * depth) ** 0.25
    bf = jnp.bfloat16
    fw = FOX_HEADS * FOX_HEAD_DIM
    hw = HGRN_HEADS * HGRN_DIM
    m = batch * seq
    n_mem = mem.shape[1]

    o_fq, o_fk, o_fv, o_ff = 0, fw, 2 * fw, 3 * fw
    o_hq = o_ff + FOX_HEADS
    o_hf, o_hi, o_hg = o_hq + hw, o_hq + 2 * hw, o_hq + 3 * hw

    tn = 1024
    ones = jnp.ones((1, tn), jnp.float32)
    qk_scale = jnp.concatenate(
        [jnp.full((1, fw), FOX_HEAD_DIM ** -0.5 * LOG2E, jnp.float32), ones], axis=1)
    hqz_scale = jnp.concatenate([jnp.full((1, hw), HGRN_DIM ** -0.5, jnp.float32), ones], axis=1)
    xq_scale = jnp.full((1, d), (d // XATTN_HEADS) ** -0.5 * LOG2E, jnp.float32)
    d_ones = jnp.ones((1, d), jnp.float32)

    xf = x.reshape(m, d)
    xb = xf.astype(bf)
    memb = mem.reshape(batch * n_mem, d).astype(bf)
    norm_w = hgrn_norm_w.reshape(depth, 1, hw)
    as_par = lambda p: p.reshape(depth, 1, -1)
    ln1 = (as_par(ln1_g), as_par(ln1_b))
    ln2 = (as_par(ln2_g), as_par(ln2_b))
    ln3 = (as_par(ln3_g), as_par(ln3_b))
    conv_b3 = as_par(conv_b)

    w_in_t = jnp.swapaxes(w_in, 1, 2)

    for l in range(depth):
        f_bias = jnp.zeros((1, FGATE_LANES), jnp.float32).at[0, :FOX_HEADS].set(fox_f_bias[l])

        qk = _matmul_nt(xb, w_in_t, l, o_fq, 2, qk_scale, bf, 1024, tn, "in_proj_qk")
        vt = _matmul_nt(xb, w_in_t, l, o_fv, 1, ones, bf, 1024, tn, "in_proj_vt",
                        transpose_out=True)
        hqz = _matmul_nt(xb, w_in_t, l, o_hq, 2, hqz_scale, jnp.float32, 1024, tn, "in_proj_hqz")
        hv = _matmul_nt(xb, w_in_t, l, o_hi, 1, ones, bf, 1024, tn, "in_proj_hv")
        hg = _matmul_nt(xb, w_in_t, l, o_hg, 1, ones, jnp.float32, 1024, tn, "in_proj_hg")
        ka, qa = _fox_bias_operands(xb, w_in_t, l, o_ff, f_bias, batch, seq)
        fox_out = _fox_attention(qk.reshape(batch, seq, 2 * fw), qa.reshape(batch, seq, fw),
                                 ka.reshape(batch, seq, fw), vt, batch, seq)
        h_out = _hgrn(hqz.reshape(batch, seq, 2 * hw), hg.reshape(batch, seq, hw),
                      hv.reshape(batch, seq, hw), hgrn_lb_logits, norm_w, l, batch, seq)
        xf, xb = _mix_out(fox_out.reshape(m, fw), h_out.reshape(m, hw), w_out, xf, *ln1, l, alpha)

        q = _matmul(xb, xq_w, l, xq_scale, bf, 1024, tn, "xattn_q")
        k = _matmul(memb, xk_w, l, d_ones, bf, 1024, tn, "xattn_k")
        v = _matmul(memb, xv_w, l, d_ones, bf, 1024, tn, "xattn_v")
        xf, xb = _xattn(q, k, v, xo_w, xf, *ln2, l, alpha, batch, seq)

        gated = _ffn_up(xb, ffn_up, conv_w, conv_b3, l, seq)
        xf, xb = _ffn_down(gated, _cast_layer(ffn_down, l), xf, *ln3, l, alpha)
    return xf.reshape(batch, seq, d)
```

```python
import functools

import jax
import jax.numpy as jnp
import numpy as np
from jax import lax
from jax.experimental import pallas as pl
from jax.experimental.pallas import tpu as pltpu

FOX_HEADS = 8
FOX_HEAD_DIM = 128
HGRN_HEADS = 8
HGRN_DIM = 128
XATTN_HEADS = 4
CONV_WIDTH = 3
LN_EPS = 1e-5
RMS_EPS = 1e-6
MASK_VALUE = -1e30
MIN_FORGET = 1e-6
LOG2E = 1.4426950408889634

V7X_VMEM_LIMIT_BYTES = 56 * 1024 * 1024
SUBLANES = 8
BF16_SUBLANES = 16
HGRN_CHUNK = 128
HGRN_DIAG = 8
LN_SUB_ROWS = 128
FGATE_LANES = 128
N_PIECES = 3


def _params(*sem):
    return pltpu.CompilerParams(dimension_semantics=sem,
                                vmem_limit_bytes=V7X_VMEM_LIMIT_BYTES)


def _nt_dot(a, b):
    return lax.dot_general(a, b, (((1,), (1,)), ((), ())),
                           preferred_element_type=jnp.float32)


def _row_slices(rows, sub=LN_SUB_ROWS):
    sub = min(sub, rows)
    return [slice(r, r + sub) for r in range(0, rows, sub)]


def _layer_norm_rows(y, g, b):
    mu = jnp.mean(y, axis=-1, keepdims=True)
    d = y - mu
    var = jnp.mean(d * d, axis=-1, keepdims=True)
    return d * lax.rsqrt(var + LN_EPS) * g + b


def _mm_kernel(a_ref, w_ref, s_ref, o_ref, wb_ref):
    @pl.when(pl.program_id(1) == 0)
    def _():
        wb_ref[...] = w_ref[...].astype(wb_ref.dtype)

    acc = jnp.dot(a_ref[...], wb_ref[...], preferred_element_type=jnp.float32)
    o_ref[...] = (acc * s_ref[...]).astype(o_ref.dtype)


def _matmul(a, w, layer, col_scale, out_dtype, tm, tn, name):
    m, k = a.shape
    n = w.shape[2]
    tm = min(tm, m)
    return pl.pallas_call(
        _mm_kernel,
        grid=(n // tn, m // tm),
        in_specs=[pl.BlockSpec((tm, k), lambda j, i: (i, 0)),
                  pl.BlockSpec((None, k, tn), lambda j, i: (layer, 0, j)),
                  pl.BlockSpec((1, tn), lambda j, i: (0, j))],
        out_specs=pl.BlockSpec((tm, tn), lambda j, i: (i, j)),
        out_shape=jax.ShapeDtypeStruct((m, n), out_dtype),
        scratch_shapes=[pltpu.VMEM((k, tn), jnp.bfloat16)],
        compiler_params=_params("arbitrary", "arbitrary"),
        name=name,
    )(a, w, col_scale)


def _mm_nt_kernel(a_ref, w_ref, s_ref, o_ref, wb_ref, *, transpose_out):
    @pl.when(pl.program_id(1) == 0)
    def _():
        wb_ref[...] = w_ref[...].astype(wb_ref.dtype)

    if transpose_out:
        o_ref[...] = _nt_dot(wb_ref[...], a_ref[...]).astype(o_ref.dtype)
    else:
        o_ref[...] = (_nt_dot(a_ref[...], wb_ref[...]) * s_ref[...]).astype(o_ref.dtype)


def _matmul_nt(a, wt, layer, row0, n_blocks, col_scale, out_dtype, tm, tn, name,
               transpose_out=False):
    m, k = a.shape
    tm = min(tm, m)
    assert row0 % SUBLANES == 0
    n = n_blocks * tn
    if transpose_out:
        out_spec = pl.BlockSpec((tn, tm), lambda j, i: (j, i))
        out_shape = jax.ShapeDtypeStruct((n, m), out_dtype)
    else:
        out_spec = pl.BlockSpec((tm, tn), lambda j, i: (i, j))
        out_shape = jax.ShapeDtypeStruct((m, n), out_dtype)
    return pl.pallas_call(
        functools.partial(_mm_nt_kernel, transpose_out=transpose_out),
        grid=(n_blocks, m // tm),
        in_specs=[pl.BlockSpec((tm, k), lambda j, i: (i, 0)),
                  pl.BlockSpec((None, pl.Element(tn), pl.Element(k)),
                               lambda j, i: (layer, pl.multiple_of(row0 + j * tn, SUBLANES), 0)),
                  pl.BlockSpec((1, tn), lambda j, i: (0, j))],
        out_specs=out_spec,
        out_shape=out_shape,
        scratch_shapes=[pltpu.VMEM((tn, k), jnp.bfloat16)],
        compiler_params=_params("arbitrary", "arbitrary"),
        name=name,
    )(a, wt, col_scale)


def _cast_kernel(w_ref, o_ref):
    o_ref[...] = w_ref[...].astype(o_ref.dtype)


def _cast_layer(w, layer, rows=512):
    _, k, n = w.shape
    return pl.pallas_call(
        _cast_kernel,
        grid=(k // rows,),
        in_specs=[pl.BlockSpec((None, rows, n), lambda r: (layer, r, 0))],
        out_specs=pl.BlockSpec((rows, n), lambda r: (r, 0)),
        out_shape=jax.ShapeDtypeStruct((k, n), jnp.bfloat16),
        compiler_params=_params("arbitrary"),
        name="cast_layer_bf16",
    )(w)


def _fgate_kernel(x_ref, wt_ref, b_ref, selk_ref, selq_ref, onek_ref, oneq_ref,
                  ka_ref, qa_ref, carry_ref, wf_ref, *, ts, heads):
    @pl.when(pl.program_id(1) == 0)
    def _():
        carry_ref[...] = jnp.zeros_like(carry_ref)
        w = wt_ref[...]
        row = lax.broadcasted_iota(jnp.int32, w.shape, 0)
        wf_ref[...] = jnp.where(row < heads, w, 0.0).astype(wf_ref.dtype)

    half = ts // 2
    halves = (slice(0, half), slice(half, ts))
    z = jnp.concatenate([_nt_dot(x_ref[sl, :], wf_ref[...]) for sl in halves], axis=0) + b_ref[...]
    lf = jnp.minimum(z, 0.0) - jnp.log(1.0 + jnp.exp(-jnp.abs(z)))
    row = lax.broadcasted_iota(jnp.int32, lf.shape, 0)
    k = 1
    while k < ts:
        lf = lf + jnp.where(row >= k, pltpu.roll(lf, k, axis=0), 0.0)
        k *= 2
    c = lf + carry_ref[0:1, :]
    carry_ref[...] = jnp.broadcast_to(c[ts - 1:ts, :], carry_ref.shape)
    c2 = c * LOG2E
    hi = c2.astype(jnp.bfloat16)
    r1 = c2 - hi.astype(jnp.float32)
    mid = r1.astype(jnp.bfloat16)
    lo = (r1 - mid.astype(jnp.float32)).astype(jnp.bfloat16)
    pieces = jnp.concatenate([hi, mid, lo], axis=1)
    for sl in halves:
        ka = jnp.dot(pieces[sl], selk_ref[...], preferred_element_type=jnp.float32) + onek_ref[...]
        qa = jnp.dot(pieces[sl], selq_ref[...], preferred_element_type=jnp.float32) + oneq_ref[...]
        ka_ref[sl, :] = ka.astype(ka_ref.dtype)
        qa_ref[sl, :] = qa.astype(qa_ref.dtype)


def _fox_bias_operands(xb, wt, layer, row0, bias, batch, seq, ts=512):
    m, d = xb.shape
    ts = min(ts, seq)
    nt = seq // ts
    lanes = FGATE_LANES
    heads = FOX_HEADS
    width = heads * FOX_HEAD_DIM
    piece = np.arange(N_PIECES)
    head = np.arange(heads)
    rows = (piece[None, :] * lanes + head[:, None]).reshape(-1)
    cols_k = (head[:, None] * FOX_HEAD_DIM + piece[None, :]).reshape(-1)
    selk = np.zeros((N_PIECES * lanes, width), np.float32)
    selq = np.zeros((N_PIECES * lanes, width), np.float32)
    onek = np.zeros((1, width), np.float32)
    oneq = np.zeros((1, width), np.float32)
    selk[rows, cols_k] = -1.0
    selq[rows, cols_k + N_PIECES] = 1.0
    onek[0, cols_k + N_PIECES] = 1.0
    oneq[0, cols_k] = 1.0
    fixed = lambda b, j: (0, 0)
    row = lambda b, j: (b * nt + j, 0)
    return pl.pallas_call(
        functools.partial(_fgate_kernel, ts=ts, heads=heads),
        grid=(batch, nt),
        in_specs=[pl.BlockSpec((ts, d), row),
                  pl.BlockSpec((None, pl.Element(lanes), pl.Element(d)),
                               lambda b, j: (layer, row0, 0)),
                  pl.BlockSpec((1, lanes), fixed),
                  pl.BlockSpec((N_PIECES * lanes, width), fixed),
                  pl.BlockSpec((N_PIECES * lanes, width), fixed),
                  pl.BlockSpec((1, width), fixed),
                  pl.BlockSpec((1, width), fixed)],
        out_specs=[pl.BlockSpec((ts, width), row), pl.BlockSpec((ts, width), row)],
        out_shape=[jax.ShapeDtypeStruct((m, width), jnp.bfloat16),
                   jax.ShapeDtypeStruct((m, width), jnp.bfloat16)],
        scratch_shapes=[pltpu.VMEM((SUBLANES, lanes), jnp.float32),
                        pltpu.VMEM((lanes, d), jnp.bfloat16)],
        compiler_params=_params("arbitrary", "arbitrary"),
        name="fox_bias_operands",
    )(xb, wt, bias, jnp.asarray(selk, jnp.bfloat16), jnp.asarray(selq, jnp.bfloat16),
      jnp.asarray(onek), jnp.asarray(oneq))


def _fox_kernel(q_ref, qa_ref, k_ref, ka_ref, vt_ref, o_ref, *score_refs, tq, tk, sub):
    sc0_ref = score_refs[:len(score_refs) // 2]
    sc1_ref = score_refs[len(score_refs) // 2:]
    i = pl.program_id(2)
    dh = q_ref.shape[1]
    nsub = tq // sub
    q_ext = [jnp.concatenate([q_ref[s * sub:(s + 1) * sub, :], qa_ref[s * sub:(s + 1) * sub, :]],
                             axis=1) for s in range(nsub)]

    def kv_block(k0, n):
        k_ext = jnp.concatenate([k_ref[pl.ds(k0, n), :], ka_ref[pl.ds(k0, n), :]], axis=1)
        return k_ext, vt_ref[:, pl.ds(k0, n)]

    def probs(carry, s):
        m, _ = carry
        m_new = jnp.maximum(m, jnp.max(s, axis=0, keepdims=True))
        p = jnp.exp2(s - m_new).astype(jnp.bfloat16)
        return m_new, jnp.exp2(m - m_new), p

    def accumulate(carry, soft, vt):
        m_new, alpha, p = soft
        vt_ones = jnp.concatenate([vt, jnp.ones((BF16_SUBLANES, vt.shape[1]), vt.dtype)], axis=0)
        acc = alpha * carry[1] + jnp.dot(vt_ones, p, preferred_element_type=jnp.float32)
        return m_new, acc

    def update(carry, s, vt):
        return accumulate(carry, probs(carry, s), vt)

    def put_scores(j, sc_ref):
        k_ext, _ = kv_block(pl.multiple_of(j * tk, tk), tk)
        for s in range(nsub):
            sc_ref[s][...] = _nt_dot(k_ext, q_ext[s])

    def soft_block(sc_ref, carries):
        return [probs(c, sc_ref[s][...]) for s, c in enumerate(carries)]

    def pv_block(j, softs, carries):
        _, vt = kv_block(pl.multiple_of(j * tk, tk), tk)
        return tuple(accumulate(c, soft, vt) for c, soft in zip(carries, softs))

    def causal(sc, k_lo, q_lo):
        key = lax.broadcasted_iota(jnp.int32, sc.shape, 0) + k_lo
        qry = lax.broadcasted_iota(jnp.int32, sc.shape, 1) + q_lo
        return jnp.where(key <= qry, sc, MASK_VALUE)

    def body(jj, carries):
        j = 2 * jj
        put_scores(j + 1, sc1_ref)
        soft = soft_block(sc0_ref, carries)
        put_scores(j + 2, sc0_ref)
        carries = pv_block(j, soft, carries)
        soft = soft_block(sc1_ref, carries)
        return pv_block(j + 1, soft, carries)

    init = tuple((jnp.full((1, sub), MASK_VALUE, jnp.float32),
                  jnp.zeros((dh + BF16_SUBLANES, sub), jnp.float32)) for _ in range(nsub))
    put_scores(0, sc0_ref)
    carries = list(lax.fori_loop(0, i * (tq // tk // 2), body, init))

    base = i * tq
    later = []
    for kb in range(1, tq // tk):
        for s in range(nsub):
            q_lo, q_hi = s * sub, (s + 1) * sub
            k_lo = kb * tk
            k_hi = min(k_lo + tk, q_hi)
            if k_hi <= k_lo:
                continue
            k_ext, vt = kv_block(pl.multiple_of(base + k_lo, sub), k_hi - k_lo)
            sc = _nt_dot(k_ext, q_ext[s])
            later.append((s, causal(sc, k_lo, q_lo) if k_hi - 1 > q_lo else sc, vt))
    _, vt = kv_block(pl.multiple_of(base, tk), tk)
    for s in range(nsub):
        sc = sc0_ref[s][...] if tk - 1 <= s * sub else causal(sc0_ref[s][...], 0, s * sub)
        carries[s] = update(carries[s], sc, vt)
    for s, sc, vt in later:
        carries[s] = update(carries[s], sc, vt)
    out = jnp.concatenate([acc[:dh] / acc[dh:dh + 1] for _, acc in carries], axis=1)
    o_ref[...] = out.T.astype(o_ref.dtype)


def _fox_attention(proj, qa, ka, vt, batch, seq, tq=1024, tk=512, sub=256):
    tq = min(tq, seq)
    assert tq % (2 * tk) == 0 and tk % sub == 0
    dh = FOX_HEAD_DIM
    h = FOX_HEADS
    score_bufs = [pltpu.VMEM((tk, sub), jnp.float32)] * (2 * (tq // sub))
    q_map = lambda b, hh, i: (b, i, hh)
    k_map = lambda b, hh, i: (b, 0, hh)
    return pl.pallas_call(
        functools.partial(_fox_kernel, tq=tq, tk=tk, sub=sub),
        grid=(batch, h, seq // tq),
        in_specs=[pl.BlockSpec((None, tq, dh), q_map),
                  pl.BlockSpec((None, tq, dh), q_map),
                  pl.BlockSpec((None, seq, dh), lambda b, hh, i: (b, 0, h + hh)),
                  pl.BlockSpec((None, seq, dh), k_map),
                  pl.BlockSpec((dh, seq), lambda b, hh, i: (hh, b))],
        out_specs=pl.BlockSpec((None, tq, dh), q_map),
        out_shape=jax.ShapeDtypeStruct((batch, seq, h * dh), jnp.bfloat16),
        scratch_shapes=score_bufs,
        compiler_params=_params("arbitrary", "arbitrary", "arbitrary"),
        name="fox_attention",
    )(proj, qa, proj, ka, vt)


def _group_ref(g, group, ref_row):
    c, d = g.shape
    g3 = g.reshape(c // group, group, d)
    return jnp.broadcast_to(g3[:, ref_row:ref_row + 1, :], g3.shape).reshape(c, d)


def _hgrn_kernel(q_ref, z_ref, g_ref, v_ref, lbl_ref, nw_ref, o_ref, st_ref,
                 *, layer, rows, chunk):
    @pl.when(pl.program_id(2) == 0)
    def _():
        st_ref[...] = jnp.zeros_like(st_ref)

    lg = lbl_ref[...]
    e = jnp.exp(lg - jnp.max(lg, axis=0, keepdims=True))
    soft = e / jnp.sum(e, axis=0, keepdims=True)
    cs = soft[0:1]
    for idx in range(1, layer + 1):
        cs = cs + soft[idx:idx + 1]
    lb = cs - soft[0:1]

    z = z_ref[...]
    ez = jnp.exp(-jnp.abs(z))
    r = 1.0 / (1.0 + ez)
    pos = z >= 0.0
    sig = jnp.where(pos, r, ez * r)
    nsig = jnp.where(pos, ez * r, r)
    f_gate = lb + (1.0 - lb) * sig
    logf = jnp.log2(jnp.maximum(f_gate, MIN_FORGET))
    k_in = (1.0 - lb) * nsig

    rt = lax.broadcasted_iota(jnp.int32, (chunk, chunk), 0)
    cc = lax.broadcasted_iota(jnp.int32, (chunk, chunk), 1)
    diag_mask = ((rt // HGRN_DIAG) == (cc // HGRN_DIAG)) & (cc <= rt)
    tri_r = lax.broadcasted_iota(jnp.int32, (chunk, N_PIECES * chunk), 0)
    tri_c = lax.broadcasted_iota(jnp.int32, (chunk, N_PIECES * chunk), 1) & (chunk - 1)
    tri = jnp.where(tri_c <= tri_r, 1.0, 0.0).astype(jnp.bfloat16)
    zeros = {}
    span = chunk // 2
    while span >= HGRN_DIAG:
        zeros[span] = jnp.zeros((chunk // (2 * span), span, HGRN_DIM), jnp.float32)
        span //= 2

    nw = nw_ref[...]
    n_chunks = rows // chunk
    slices = [slice(c * chunk, (c + 1) * chunk) for c in range(n_chunks)]

    hi = logf.astype(jnp.bfloat16)
    r1 = logf - hi.astype(jnp.float32)
    mid = r1.astype(jnp.bfloat16)
    lo = (r1 - mid.astype(jnp.float32)).astype(jnp.bfloat16)
    gcs = [jnp.dot(tri, jnp.concatenate([hi[sl], mid[sl], lo[sl]], axis=0),
                   preferred_element_type=jnp.float32) for sl in slices]

    intra = []
    for c, sl in enumerate(slices):
        gc = gcs[c]
        qc = q_ref[sl, :]
        kc = k_in[sl]

        a = None
        span = chunk // 2
        while span >= HGRN_DIAG:
            group = 2 * span
            ng = chunk // group
            g3 = gc.reshape(ng, group, HGRN_DIM)
            gref = g3[:, span:span + 1, :]
            q_up = qc.reshape(ng, group, HGRN_DIM)[:, span:, :] * jnp.exp2(g3[:, span:, :] - gref)
            k_lo = kc.reshape(ng, group, HGRN_DIM)[:, :span, :] * jnp.exp2(gref - g3[:, :span, :])
            ql = jnp.concatenate([zeros[span], q_up], axis=1).reshape(chunk, HGRN_DIM)
            kl = jnp.concatenate([k_lo, zeros[span]], axis=1).reshape(chunk, HGRN_DIM)
            p = _nt_dot(ql.astype(jnp.bfloat16), kl.astype(jnp.bfloat16))
            a = p if a is None else jnp.where((rt // group) == (cc // group), p, a)
            span //= 2
        dref = gc - _group_ref(gc, HGRN_DIAG, HGRN_DIAG // 2)
        p = _nt_dot((qc * jnp.exp2(dref)).astype(jnp.bfloat16),
                    (kc * jnp.exp2(-dref)).astype(jnp.bfloat16))
        intra.append(jnp.where(diag_mask, p, a).astype(jnp.bfloat16))

    o_intra, upds, qgs, decs = [], [], [], []
    for c, sl in enumerate(slices):
        gc = gcs[c]
        vc = v_ref[sl, :]
        g_last = gc[chunk - 1:chunk]
        k_dec = (k_in[sl] * jnp.exp2(g_last - gc)).astype(jnp.bfloat16)
        o_intra.append(jnp.dot(intra[c], vc, preferred_element_type=jnp.float32))
        upds.append(lax.dot_general(vc, k_dec, (((0,), (0,)), ((), ())),
                                    preferred_element_type=jnp.float32))
        qgs.append((q_ref[sl, :] * jnp.exp2(gc)).astype(jnp.bfloat16))
        decs.append(jnp.exp2(g_last))

    st = st_ref[...]
    for c, sl in enumerate(slices):
        o = o_intra[c] + _nt_dot(qgs[c], st.astype(jnp.bfloat16))
        st = st * decs[c] + upds[c]

        ms = jnp.mean(o * o, axis=-1, keepdims=True)
        y = o * lax.rsqrt(ms + RMS_EPS) * nw
        gate = g_ref[sl, :]
        y = y * (gate * (1.0 / (1.0 + jnp.exp(-gate))))
        o_ref[sl, :] = y.astype(o_ref.dtype)
    st_ref[...] = st


def _hgrn(hqz, hg, hv, lb_logits, norm_w, layer, batch, seq, rows=512):
    rows = min(rows, seq)
    h = HGRN_HEADS
    d = HGRN_DIM
    depth = lb_logits.shape[0]
    head = lambda b, hh, r: (b, r, hh)
    return pl.pallas_call(
        functools.partial(_hgrn_kernel, layer=layer, rows=rows, chunk=HGRN_CHUNK),
        grid=(batch, h, seq // rows),
        in_specs=[pl.BlockSpec((None, rows, d), head),
                  pl.BlockSpec((None, rows, d), lambda b, hh, r: (b, r, h + hh)),
                  pl.BlockSpec((None, rows, d), head),
                  pl.BlockSpec((None, rows, d), head),
                  pl.BlockSpec((depth, d), lambda b, hh, r: (0, hh)),
                  pl.BlockSpec((None, 1, d), lambda b, hh, r: (layer, 0, hh))],
        out_specs=pl.BlockSpec((None, rows, d), head),
        out_shape=jax.ShapeDtypeStruct((batch, seq, h * d), jnp.bfloat16),
        scratch_shapes=[pltpu.VMEM((d, d), jnp.float32)],
        compiler_params=_params("arbitrary", "arbitrary", "arbitrary"),
        name="hgrn2",
    )(hqz, hqz, hg, hv, lb_logits, norm_w)


def _mix_out_kernel(a1_ref, a2_ref, w1_ref, w2_ref, x_ref, g_ref, b_ref, o_ref, ob_ref,
                    wb_ref, *, alpha):
    @pl.when(pl.program_id(0) == 0)
    def _():
        wb_ref[0] = w1_ref[...].astype(wb_ref.dtype)
        wb_ref[1] = w2_ref[...].astype(wb_ref.dtype)

    for sl in _row_slices(x_ref.shape[0]):
        acc = jnp.dot(a1_ref[sl, :], wb_ref[0], preferred_element_type=jnp.float32)
        acc = acc + jnp.dot(a2_ref[sl, :], wb_ref[1], preferred_element_type=jnp.float32)
        o = _layer_norm_rows(alpha * x_ref[sl, :] + acc, g_ref[...], b_ref[...])
        o_ref[sl, :] = o
        ob_ref[sl, :] = o.astype(ob_ref.dtype)


def _resident(block_shape, index_map):
    return pl.BlockSpec(block_shape, index_map, pipeline_mode=pl.Buffered(1))


def _mix_out(a1, a2, w, x, g, b, layer, alpha, tm=512):
    m, d = x.shape
    k = a1.shape[1]
    tm = min(tm, m)
    row = lambda i: (i, 0)
    par = lambda i: (layer, 0, 0)
    return pl.pallas_call(
        functools.partial(_mix_out_kernel, alpha=alpha),
        grid=(m // tm,),
        in_specs=[pl.BlockSpec((tm, k), row), pl.BlockSpec((tm, k), row),
                  _resident((None, k, d), par), _resident((None, k, d), lambda i: (layer, 1, 0)),
                  pl.BlockSpec((tm, d), row),
                  _resident((None, 1, d), par), _resident((None, 1, d), par)],
        out_specs=[pl.BlockSpec((tm, d), row), pl.BlockSpec((tm, d), row)],
        out_shape=[jax.ShapeDtypeStruct((m, d), jnp.float32),
                   jax.ShapeDtypeStruct((m, d), jnp.bfloat16)],
        scratch_shapes=[pltpu.VMEM((2, k, d), jnp.bfloat16)],
        compiler_params=_params("arbitrary"),
        name="mix_out_ln",
    )(a1, a2, w, w, x, g, b)


def _xattn_kernel(q_ref, k_ref, v_ref, wo_ref, x_ref, g_ref, b_ref, o_ref, ob_ref,
                  wb_ref, *, alpha, heads):
    @pl.when(pl.program_id(0) == 0)
    def _():
        wb_ref[...] = wo_ref[...].astype(wb_ref.dtype)

    d = q_ref.shape[1]
    dh = d // heads
    outs = []
    scores = [_nt_dot(q_ref[:, h * dh:(h + 1) * dh], k_ref[:, h * dh:(h + 1) * dh])
              for h in range(heads)]
    for h in range(heads):
        sl = slice(h * dh, (h + 1) * dh)
        s = scores[h]
        p = jnp.exp2(s - jnp.max(s, axis=-1, keepdims=True))
        p = p / jnp.sum(p, axis=-1, keepdims=True)
        outs.append(jnp.dot(p.astype(jnp.bfloat16), v_ref[:, sl],
                            preferred_element_type=jnp.float32).astype(jnp.bfloat16))
    att = jnp.concatenate(outs, axis=1)
    for sl in _row_slices(x_ref.shape[0]):
        acc = jnp.dot(att[sl], wb_ref[...], preferred_element_type=jnp.float32)
        o = _layer_norm_rows(alpha * x_ref[sl, :] + acc, g_ref[...], b_ref[...])
        o_ref[sl, :] = o
        ob_ref[sl, :] = o.astype(ob_ref.dtype)


def _xattn(q, k, v, wo, x, g, b, layer, alpha, batch, seq, tm=256):
    m, d = x.shape
    n_mem = k.shape[0] // batch
    tm = min(tm, seq)
    per_b = seq // tm
    row = lambda i: (i, 0)
    par = lambda i: (layer, 0, 0)
    mem_map = lambda i: (i // per_b, 0)
    return pl.pallas_call(
        functools.partial(_xattn_kernel, alpha=alpha, heads=XATTN_HEADS),
        grid=(m // tm,),
        in_specs=[pl.BlockSpec((tm, d), row),
                  pl.BlockSpec((n_mem, d), mem_map),
                  pl.BlockSpec((n_mem, d), mem_map),
                  _resident((None, d, d), par),
                  pl.BlockSpec((tm, d), row),
                  _resident((None, 1, d), par), _resident((None, 1, d), par)],
        out_specs=[pl.BlockSpec((tm, d), row), pl.BlockSpec((tm, d), row)],
        out_shape=[jax.ShapeDtypeStruct((m, d), jnp.float32),
                   jax.ShapeDtypeStruct((m, d), jnp.bfloat16)],
        scratch_shapes=[pltpu.VMEM((d, d), jnp.bfloat16)],
        compiler_params=_params("arbitrary"),
        name="xattn_out_ln",
    )(q, k, v, wo, x, g, b)


def _ffn_up_kernel(xb_ref, wa_ref, wb_ref, cwa_ref, cwb_ref, cba_ref, cbb_ref, o_ref,
                   wab_ref, halo_ref, *, tiles_per_seq, rs):
    i = pl.program_id(1)
    tm = xb_ref.shape[0]

    @pl.when(i == 0)
    def _():
        wab_ref[0] = wa_ref[...].astype(wab_ref.dtype)
        wab_ref[1] = wb_ref[...].astype(wab_ref.dtype)

    @pl.when(i % tiles_per_seq == 0)
    def _():
        halo_ref[...] = jnp.zeros_like(halo_ref)

    cws = (cwa_ref[...], cwb_ref[...])
    cbs = (cba_ref[...], cbb_ref[...])
    prev = [halo_ref[0], halo_ref[1]]
    for r in range(tm // rs):
        xr = xb_ref[r * rs:(r + 1) * rs, :]
        branches = []
        for t in range(2):
            h = jnp.dot(xr, wab_ref[t], preferred_element_type=jnp.float32)
            seam = jnp.concatenate([prev[t], h[:SUBLANES]], axis=0)
            h1 = jnp.concatenate([seam[SUBLANES - 1:2 * SUBLANES - 1],
                                  pltpu.roll(h, 1, axis=0)[SUBLANES:]], axis=0)
            h2 = jnp.concatenate([seam[SUBLANES - 2:2 * SUBLANES - 2],
                                  pltpu.roll(h, 2, axis=0)[SUBLANES:]], axis=0)
            prev[t] = h[rs - SUBLANES:rs]
            cw = cws[t]
            branches.append(cw[0:1] * h2 + cw[1:2] * h1 + cw[2:3] * h + cbs[t])
        a, bb = branches
        gated = (a * (1.0 / (1.0 + jnp.exp(-a)))) * bb
        o_ref[r * rs:(r + 1) * rs, :] = gated.astype(o_ref.dtype)
    halo_ref[0] = prev[0]
    halo_ref[1] = prev[1]


def _ffn_up(xb, w_up, conv_w, conv_b, layer, seq, tm=2048, tf=512, rs=256):
    m, d = xb.shape
    dff = w_up.shape[2] // 2
    tm = min(tm, seq)
    nj = dff // tf
    col_a = lambda j, i: (layer, 0, j)
    col_b = lambda j, i: (layer, 0, nj + j)
    return pl.pallas_call(
        functools.partial(_ffn_up_kernel, tiles_per_seq=seq // tm, rs=rs),
        grid=(nj, m // tm),
        in_specs=[pl.BlockSpec((tm, d), lambda j, i: (i, 0)),
                  pl.BlockSpec((None, d, tf), col_a), pl.BlockSpec((None, d, tf), col_b),
                  pl.BlockSpec((None, CONV_WIDTH, tf), col_a),
                  pl.BlockSpec((None, CONV_WIDTH, tf), col_b),
                  pl.BlockSpec((None, 1, tf), col_a), pl.BlockSpec((None, 1, tf), col_b)],
        out_specs=pl.BlockSpec((tm, tf), lambda j, i: (i, j)),
        out_shape=jax.ShapeDtypeStruct((m, dff), jnp.bfloat16),
        scratch_shapes=[pltpu.VMEM((2, d, tf), jnp.bfloat16),
                        pltpu.VMEM((2, SUBLANES, tf), jnp.float32)],
        compiler_params=_params("arbitrary", "arbitrary"),
        name="ffn_up_conv_gate",
    )(xb, w_up, w_up, conv_w, conv_w, conv_b, conv_b)


def _ffn_down_kernel(a_ref, w_ref, x_ref, g_ref, b_ref, o_ref, ob_ref, *, alpha):
    for sl in _row_slices(x_ref.shape[0]):
        acc = jnp.dot(a_ref[sl, :], w_ref[...], preferred_element_type=jnp.float32)
        o = _layer_norm_rows(alpha * x_ref[sl, :] + acc, g_ref[...], b_ref[...])
        o_ref[sl, :] = o
        ob_ref[sl, :] = o.astype(ob_ref.dtype)


def _ffn_down(a, w, x, g, b, layer, alpha, tm=256):
    m, d = x.shape
    f = a.shape[1]
    tm = min(tm, m)
    row = lambda i: (i, 0)
    par = lambda i: (layer, 0, 0)
    return pl.pallas_call(
        functools.partial(_ffn_down_kernel, alpha=alpha),
        grid=(m // tm,),
        in_specs=[pl.BlockSpec((tm, f), row),
                  _resident((f, d), lambda i: (0, 0)),
                  pl.BlockSpec((tm, d), row),
                  _resident((None, 1, d), par), _resident((None, 1, d), par)],
        out_specs=[pl.BlockSpec((tm, d), row), pl.BlockSpec((tm, d), row)],
        out_shape=[jax.ShapeDtypeStruct((m, d), jnp.float32),
                   jax.ShapeDtypeStruct((m, d), jnp.bfloat16)],
        compiler_params=_params("arbitrary"),
        name="ffn_down_ln",
    )(a, w, x, g, b)


def kernel(x, mem, w_in, fox_f_bias, hgrn_lb_logits, hgrn_norm_w, w_out, ln1_g, ln1_b,
           xq_w, xk_w, xv_w, xo_w, ln2_g, ln2_b, ffn_up, conv_w, conv_b, ffn_down,
           ln3_g, ln3_b):
    batch, seq, d = x.shape
    depth = w_in.shape[0]
    alpha = (2 * depth) ** 0.25
    bf = jnp.bfloat16
    fw = FOX_HEADS * FOX_HEAD_DIM
    hw = HGRN_HEADS * HGRN_DIM
    m = batch * seq
    n_mem = mem.shape[1]

    o_fq, o_fk, o_fv, o_ff = 0, fw, 2 * fw, 3 * fw
    o_hq = o_ff + FOX_HEADS
    o_hf, o_hi, o_hg = o_hq + hw, o_hq + 2 * hw, o_hq + 3 * hw

    tn = 1024
    ones = jnp.ones((1, tn), jnp.float32)
    qk_scale = jnp.concatenate(
        [jnp.full((1, fw), FOX_HEAD_DIM ** -0.5 * LOG2E, jnp.float32), ones], axis=1)
    hqz_scale = jnp.concatenate([jnp.full((1, hw), HGRN_DIM ** -0.5, jnp.float32), ones], axis=1)
    xq_scale = jnp.full((1, d), (d // XATTN_HEADS) ** -0.5 * LOG2E, jnp.float32)
    d_ones = jnp.ones((1, d), jnp.float32)

    xf = x.reshape(m, d)
    xb = xf.astype(bf)
    memb = mem.reshape(batch * n_mem, d).astype(bf)
    norm_w = hgrn_norm_w.reshape(depth, 1, hw)
    as_par = lambda p: p.reshape(depth, 1, -1)
    ln1 = (as_par(ln1_g), as_par(ln1_b))
    ln2 = (as_par(ln2_g), as_par(ln2_b))
    ln3 = (as_par(ln3_g), as_par(ln3_b))
    conv_b3 = as_par(conv_b)

    w_in_t = jnp.swapaxes(w_in, 1, 2)

    for l in range(depth):
        f_bias = jnp.zeros((1, FGATE_LANES), jnp.float32).at[0, :FOX_HEADS].set(fox_f_bias[l])

        qk = _matmul_nt(xb, w_in_t, l, o_fq, 2, qk_scale, bf, 1024, tn, "in_proj_qk")
        vt = _matmul_nt(xb, w_in_t, l, o_fv, 1, ones, bf, 1024, tn, "in_proj_vt",
                        transpose_out=True)
        hqz = _matmul_nt(xb, w_in_t, l, o_hq, 2, hqz_scale, jnp.float32, 1024, tn, "in_proj_hqz")
        hv = _matmul_nt(xb, w_in_t, l, o_hi, 1, ones, bf, 1024, tn, "in_proj_hv")
        hg = _matmul_nt(xb, w_in_t, l, o_hg, 1, ones, jnp.float32, 1024, tn, "in_proj_hg")
        ka, qa = _fox_bias_operands(xb, w_in_t, l, o_ff, f_bias, batch, seq)
        fox_out = _fox_attention(qk.reshape(batch, seq, 2 * fw), qa.reshape(batch, seq, fw),
                                 ka.reshape(batch, seq, fw), vt, batch, seq)
        h_out = _hgrn(hqz.reshape(batch, seq, 2 * hw), hg.reshape(batch, seq, hw),
                      hv.reshape(batch, seq, hw), hgrn_lb_logits, norm_w, l, batch, seq)
        xf, xb = _mix_out(fox_out.reshape(m, fw), h_out.reshape(m, hw), w_out, xf, *ln1, l, alpha)

        q = _matmul(xb, xq_w, l, xq_scale, bf, 1024, tn, "xattn_q")
        k = _matmul(memb, xk_w, l, d_ones, bf, 1024, tn, "xattn_k")
        v = _matmul(memb, xv_w, l, d_ones, bf, 1024, tn, "xattn_v")
        xf, xb = _xattn(q, k, v, xo_w, xf, *ln2, l, alpha, batch, seq)

        gated = _ffn_up(xb, ffn_up, conv_w, conv_b3, l, seq)
        xf, xb = _ffn_down(gated, _cast_layer(ffn_down, l), xf, *ln3, l, alpha)
    return xf.reshape(batch, seq, d)
```

```python
import functools

import jax
import jax.numpy as jnp
import numpy as np
from jax import lax
from jax.experimental import pallas as pl
from jax.experimental.pallas import tpu as pltpu

FOX_HEADS = 8
FOX_HEAD_DIM = 128
HGRN_HEADS = 8
HGRN_DIM = 128
XATTN_HEADS = 4
CONV_WIDTH = 3
LN_EPS = 1e-5
RMS_EPS = 1e-6
MASK_VALUE = -1e30
MIN_FORGET = 1e-6
LOG2E = 1.4426950408889634

V7X_VMEM_LIMIT_BYTES = 56 * 1024 * 1024
SUBLANES = 8
BF16_SUBLANES = 16
HGRN_CHUNK = 128
HGRN_DIAG = 8
LN_SUB_ROWS = 128
FGATE_LANES = 128
N_PIECES = 3


def _params(*sem):
    return pltpu.CompilerParams(dimension_semantics=sem,
                                vmem_limit_bytes=V7X_VMEM_LIMIT_BYTES)


def _nt_dot(a, b):
    return lax.dot_general(a, b, (((1,), (1,)), ((), ())),
                           preferred_element_type=jnp.float32)


def _row_slices(rows, sub=LN_SUB_ROWS):
    sub = min(sub, rows)
    return [slice(r, r + sub) for r in range(0, rows, sub)]


def _layer_norm_rows(y, g, b):
    mu = jnp.mean(y, axis=-1, keepdims=True)
    d = y - mu
    var = jnp.mean(d * d, axis=-1, keepdims=True)
    return d * lax.rsqrt(var + LN_EPS) * g + b


def _mm_kernel(a_ref, w_ref, s_ref, o_ref, wb_ref):
    @pl.when(pl.program_id(1) == 0)
    def _():
        wb_ref[...] = w_ref[...].astype(wb_ref.dtype)

    acc = jnp.dot(a_ref[...], wb_ref[...], preferred_element_type=jnp.float32)
    o_ref[...] = (acc * s_ref[...]).astype(o_ref.dtype)


def _matmul(a, w, layer, col_scale, out_dtype, tm, tn, name):
    m, k = a.shape
    n = w.shape[2]
    tm = min(tm, m)
    return pl.pallas_call(
        _mm_kernel,
        grid=(n // tn, m // tm),
        in_specs=[pl.BlockSpec((tm, k), lambda j, i: (i, 0)),
                  pl.BlockSpec((None, k, tn), lambda j, i: (layer, 0, j)),
                  pl.BlockSpec((1, tn), lambda j, i: (0, j))],
        out_specs=pl.BlockSpec((tm, tn), lambda j, i: (i, j)),
        out_shape=jax.ShapeDtypeStruct((m, n), out_dtype),
        scratch_shapes=[pltpu.VMEM((k, tn), jnp.bfloat16)],
        compiler_params=_params("arbitrary", "arbitrary"),
        name=name,
    )(a, w, col_scale)


def _mm_nt_kernel(a_ref, w_ref, s_ref, o_ref, wb_ref, *, transpose_out):
    @pl.when(pl.program_id(1) == 0)
    def _():
        wb_ref[...] = w_ref[...].astype(wb_ref.dtype)

    if transpose_out:
        o_ref[...] = _nt_dot(wb_ref[...], a_ref[...]).astype(o_ref.dtype)
    else:
        o_ref[...] = (_nt_dot(a_ref[...], wb_ref[...]) * s_ref[...]).astype(o_ref.dtype)


def _matmul_nt(a, wt, layer, row0, n_blocks, col_scale, out_dtype, tm, tn, name,
               transpose_out=False):
    m, k = a.shape
    tm = min(tm, m)
    assert row0 % SUBLANES == 0
    n = n_blocks * tn
    if transpose_out:
        out_spec = pl.BlockSpec((tn, tm), lambda j, i: (j, i))
        out_shape = jax.ShapeDtypeStruct((n, m), out_dtype)
    else:
        out_spec = pl.BlockSpec((tm, tn), lambda j, i: (i, j))
        out_shape = jax.ShapeDtypeStruct((m, n), out_dtype)
    return pl.pallas_call(
        functools.partial(_mm_nt_kernel, transpose_out=transpose_out),
        grid=(n_blocks, m // tm),
        in_specs=[pl.BlockSpec((tm, k), lambda j, i: (i, 0)),
                  pl.BlockSpec((None, pl.Element(tn), pl.Element(k)),
                               lambda j, i: (layer, pl.multiple_of(row0 + j * tn, SUBLANES), 0)),
                  pl.BlockSpec((1, tn), lambda j, i: (0, j))],
        out_specs=out_spec,
        out_shape=out_shape,
        scratch_shapes=[pltpu.VMEM((tn, k), jnp.bfloat16)],
        compiler_params=_params("arbitrary", "arbitrary"),
        name=name,
    )(a, wt, col_scale)


def _cast_kernel(w_ref, o_ref):
    o_ref[...] = w_ref[...].astype(o_ref.dtype)


def _cast_layer(w, layer, rows=512):
    _, k, n = w.shape
    return pl.pallas_call(
        _cast_kernel,
        grid=(k // rows,),
        in_specs=[pl.BlockSpec((None, rows, n), lambda r: (layer, r, 0))],
        out_specs=pl.BlockSpec((rows, n), lambda r: (r, 0)),
        out_shape=jax.ShapeDtypeStruct((k, n), jnp.bfloat16),
        compiler_params=_params("arbitrary"),
        name="cast_layer_bf16",
    )(w)


def _fgate_kernel(x_ref, wt_ref, b_ref, selk_ref, selq_ref, onek_ref, oneq_ref,
                  ka_ref, qa_ref, carry_ref, wf_ref, *, ts, heads):
    @pl.when(pl.program_id(1) == 0)
    def _():
        carry_ref[...] = jnp.zeros_like(carry_ref)
        w = wt_ref[...]
        row = lax.broadcasted_iota(jnp.int32, w.shape, 0)
        wf_ref[...] = jnp.where(row < heads, w, 0.0).astype(wf_ref.dtype)

    half = ts // 2
    halves = (slice(0, half), slice(half, ts))
    z = jnp.concatenate([_nt_dot(x_ref[sl, :], wf_ref[...]) for sl in halves], axis=0) + b_ref[...]
    lf = jnp.minimum(z, 0.0) - jnp.log(1.0 + jnp.exp(-jnp.abs(z)))
    row = lax.broadcasted_iota(jnp.int32, lf.shape, 0)
    k = 1
    while k < ts:
        lf = lf + jnp.where(row >= k, pltpu.roll(lf, k, axis=0), 0.0)
        k *= 2
    c = lf + carry_ref[0:1, :]
    carry_ref[...] = jnp.broadcast_to(c[ts - 1:ts, :], carry_ref.shape)
    c2 = c * LOG2E
    hi = c2.astype(jnp.bfloat16)
    r1 = c2 - hi.astype(jnp.float32)
    mid = r1.astype(jnp.bfloat16)
    lo = (r1 - mid.astype(jnp.float32)).astype(jnp.bfloat16)
    pieces = jnp.concatenate([hi, mid, lo], axis=1)
    for sl in halves:
        ka = jnp.dot(pieces[sl], selk_ref[...], preferred_element_type=jnp.float32) + onek_ref[...]
        qa = jnp.dot(pieces[sl], selq_ref[...], preferred_element_type=jnp.float32) + oneq_ref[...]
        ka_ref[sl, :] = ka.astype(ka_ref.dtype)
        qa_ref[sl, :] = qa.astype(qa_ref.dtype)


def _fox_bias_operands(xb, wt, layer, row0, bias, batch, seq, ts=512):
    m, d = xb.shape
    ts = min(ts, seq)
    nt = seq // ts
    lanes = FGATE_LANES
    heads = FOX_HEADS
    width = heads * FOX_HEAD_DIM
    piece = np.arange(N_PIECES)
    head = np.arange(heads)
    rows = (piece[None, :] * lanes + head[:, None]).reshape(-1)
    cols_k = (head[:, None] * FOX_HEAD_DIM + piece[None, :]).reshape(-1)
    selk = np.zeros((N_PIECES * lanes, width), np.float32)
    selq = np.zeros((N_PIECES * lanes, width), np.float32)
    onek = np.zeros((1, width), np.float32)
    oneq = np.zeros((1, width), np.float32)
    selk[rows, cols_k] = -1.0
    selq[rows, cols_k + N_PIECES] = 1.0
    onek[0, cols_k + N_PIECES] = 1.0
    oneq[0, cols_k] = 1.0
    fixed = lambda b, j: (0, 0)
    row = lambda b, j: (b * nt + j, 0)
    return pl.pallas_call(
        functools.partial(_fgate_kernel, ts=ts, heads=heads),
        grid=(batch, nt),
        in_specs=[pl.BlockSpec((ts, d), row),
                  pl.BlockSpec((None, pl.Element(lanes), pl.Element(d)),
                               lambda b, j: (layer, row0, 0)),
                  pl.BlockSpec((1, lanes), fixed),
                  pl.BlockSpec((N_PIECES * lanes, width), fixed),
                  pl.BlockSpec((N_PIECES * lanes, width), fixed),
                  pl.BlockSpec((1, width), fixed),
                  pl.BlockSpec((1, width), fixed)],
        out_specs=[pl.BlockSpec((ts, width), row), pl.BlockSpec((ts, width), row)],
        out_shape=[jax.ShapeDtypeStruct((m, width), jnp.bfloat16),
                   jax.ShapeDtypeStruct((m, width), jnp.bfloat16)],
        scratch_shapes=[pltpu.VMEM((SUBLANES, lanes), jnp.float32),
                        pltpu.VMEM((lanes, d), jnp.bfloat16)],
        compiler_params=_params("arbitrary", "arbitrary"),
        name="fox_bias_operands",
    )(xb, wt, bias, jnp.asarray(selk, jnp.bfloat16), jnp.asarray(selq, jnp.bfloat16),
      jnp.asarray(onek), jnp.asarray(oneq))


def _fox_kernel(q_ref, qa_ref, k_ref, ka_ref, vt_ref, o_ref, *score_refs, tq, tk, sub):
    sc0_ref = score_refs[:len(score_refs) // 2]
    sc1_ref = score_refs[len(score_refs) // 2:]
    i = pl.program_id(2)
    dh = q_ref.shape[1]
    nsub = tq // sub
    q_ext = [jnp.concatenate([q_ref[s * sub:(s + 1) * sub, :], qa_ref[s * sub:(s + 1) * sub, :]],
                             axis=1) for s in range(nsub)]

    def kv_block(k0, n):
        k_ext = jnp.concatenate([k_ref[pl.ds(k0, n), :], ka_ref[pl.ds(k0, n), :]], axis=1)
        return k_ext, vt_ref[:, pl.ds(k0, n)]

    def probs(carry, s):
        m, _ = carry
        m_new = jnp.maximum(m, jnp.max(s, axis=0, keepdims=True))
        p = jnp.exp2(s - m_new).astype(jnp.bfloat16)
        return m_new, jnp.exp2(m - m_new), p

    def accumulate(carry, soft, vt):
        m_new, alpha, p = soft
        vt_ones = jnp.concatenate([vt, jnp.ones((BF16_SUBLANES, vt.shape[1]), vt.dtype)], axis=0)
        acc = alpha * carry[1] + jnp.dot(vt_ones, p, preferred_element_type=jnp.float32)
        return m_new, acc

    def update(carry, s, vt):
        return accumulate(carry, probs(carry, s), vt)

    def put_scores(j, sc_ref):
        k_ext, _ = kv_block(pl.multiple_of(j * tk, tk), tk)
        for s in range(nsub):
            sc_ref[s][...] = _nt_dot(k_ext, q_ext[s])

    def soft_block(sc_ref, carries):
        return [probs(c, sc_ref[s][...]) for s, c in enumerate(carries)]

    def pv_block(j, softs, carries):
        _, vt = kv_block(pl.multiple_of(j * tk, tk), tk)
        return tuple(accumulate(c, soft, vt) for c, soft in zip(carries, softs))

    def causal(sc, k_lo, q_lo):
        key = lax.broadcasted_iota(jnp.int32, sc.shape, 0) + k_lo
        qry = lax.broadcasted_iota(jnp.int32, sc.shape, 1) + q_lo
        return jnp.where(key <= qry, sc, MASK_VALUE)

    def body(jj, carries):
        j = 2 * jj
        put_scores(j + 1, sc1_ref)
        soft = soft_block(sc0_ref, carries)
        put_scores(j + 2, sc0_ref)
        carries = pv_block(j, soft, carries)
        soft = soft_block(sc1_ref, carries)
        return pv_block(j + 1, soft, carries)

    init = tuple((jnp.full((1, sub), MASK_VALUE, jnp.float32),
                  jnp.zeros((dh + BF16_SUBLANES, sub), jnp.float32)) for _ in range(nsub))
    put_scores(0, sc0_ref)
    carries = list(lax.fori_loop(0, i * (tq // tk // 2), body, init))

    base = i * tq
    later = []
    for kb in range(1, tq // tk):
        for s in range(nsub):
            q_lo, q_hi = s * sub, (s + 1) * sub
            k_lo = kb * tk
            k_hi = min(k_lo + tk, q_hi)
            if k_hi <= k_lo:
                continue
            k_ext, vt = kv_block(pl.multiple_of(base + k_lo, sub), k_hi - k_lo)
            sc = _nt_dot(k_ext, q_ext[s])
            later.append((s, causal(sc, k_lo, q_lo) if k_hi - 1 > q_lo else sc, vt))
    _, vt = kv_block(pl.multiple_of(base, tk), tk)
    for s in range(nsub):
        sc = sc0_ref[s][...] if tk - 1 <= s * sub else causal(sc0_ref[s][...], 0, s * sub)
        carries[s] = update(carries[s], sc, vt)
    for s, sc, vt in later:
        carries[s] = update(carries[s], sc, vt)
    out = jnp.concatenate([acc[:dh] / acc[dh:dh + 1] for _, acc in carries], axis=1)
    o_ref[...] = out.T.astype(o_ref.dtype)


def _fox_attention(proj, qa, ka, vt, batch, seq, tq=1024, tk=512, sub=256):
    tq = min(tq, seq)
    assert tq % (2 * tk) == 0 and tk % sub == 0
    dh = FOX_HEAD_DIM
    h = FOX_HEADS
    score_bufs = [pltpu.VMEM((tk, sub), jnp.float32)] * (2 * (tq // sub))
    q_map = lambda b, hh, i: (b, i, hh)
    k_map = lambda b, hh, i: (b, 0, hh)
    return pl.pallas_call(
        functools.partial(_fox_kernel, tq=tq, tk=tk, sub=sub),
        grid=(batch, h, seq // tq),
        in_specs=[pl.BlockSpec((None, tq, dh), q_map),
                  pl.BlockSpec((None, tq, dh), q_map),
                  pl.BlockSpec((None, seq, dh), lambda b, hh, i: (b, 0, h + hh)),
                  pl.BlockSpec((None, seq, dh), k_map),
                  pl.BlockSpec((dh, seq), lambda b, hh, i: (hh, b))],
        out_specs=pl.BlockSpec((None, tq, dh), q_map),
        out_shape=jax.ShapeDtypeStruct((batch, seq, h * dh), jnp.bfloat16),
        scratch_shapes=score_bufs,
        compiler_params=_params("arbitrary", "arbitrary", "arbitrary"),
        name="fox_attention",
    )(proj, qa, proj, ka, vt)


def _group_ref(g, group, ref_row):
    c, d = g.shape
    g3 = g.reshape(c // group, group, d)
    return jnp.broadcast_to(g3[:, ref_row:ref_row + 1, :], g3.shape).reshape(c, d)


def _hgrn_kernel(q_ref, z_ref, g_ref, v_ref, lbl_ref, nw_ref, o_ref, st_ref,
                 *, layer, rows, chunk):
    @pl.when(pl.program_id(2) == 0)
    def _():
        st_ref[...] = jnp.zeros_like(st_ref)

    lg = lbl_ref[...]
    e = jnp.exp(lg - jnp.max(lg, axis=0, keepdims=True))
    soft = e / jnp.sum(e, axis=0, keepdims=True)
    cs = soft[0:1]
    for idx in range(1, layer + 1):
        cs = cs + soft[idx:idx + 1]
    lb = cs - soft[0:1]

    z = z_ref[...]
    ez = jnp.exp(-jnp.abs(z))
    r = 1.0 / (1.0 + ez)
    pos = z >= 0.0
    sig = jnp.where(pos, r, ez * r)
    nsig = jnp.where(pos, ez * r, r)
    f_gate = lb + (1.0 - lb) * sig
    logf = jnp.log2(jnp.maximum(f_gate, MIN_FORGET))
    k_in = (1.0 - lb) * nsig

    rt = lax.broadcasted_iota(jnp.int32, (chunk, chunk), 0)
    cc = lax.broadcasted_iota(jnp.int32, (chunk, chunk), 1)
    diag_mask = ((rt // HGRN_DIAG) == (cc // HGRN_DIAG)) & (cc <= rt)
    tri_r = lax.broadcasted_iota(jnp.int32, (chunk, N_PIECES * chunk), 0)
    tri_c = lax.broadcasted_iota(jnp.int32, (chunk, N_PIECES * chunk), 1) & (chunk - 1)
    tri = jnp.where(tri_c <= tri_r, 1.0, 0.0).astype(jnp.bfloat16)
    zeros = {}
    span = chunk // 2
    while span >= HGRN_DIAG:
        zeros[span] = jnp.zeros((chunk // (2 * span), span, HGRN_DIM), jnp.float32)
        span //= 2

    nw = nw_ref[...]
    n_chunks = rows // chunk
    slices = [slice(c * chunk, (c + 1) * chunk) for c in range(n_chunks)]

    hi = logf.astype(jnp.bfloat16)
    r1 = logf - hi.astype(jnp.float32)
    mid = r1.astype(jnp.bfloat16)
    lo = (r1 - mid.astype(jnp.float32)).astype(jnp.bfloat16)
    gcs = [jnp.dot(tri, jnp.concatenate([hi[sl], mid[sl], lo[sl]], axis=0),
                   preferred_element_type=jnp.float32) for sl in slices]

    intra = []
    for c, sl in enumerate(slices):
        gc = gcs[c]
        qc = q_ref[sl, :]
        kc = k_in[sl]

        a = None
        span = chunk // 2
        while span >= HGRN_DIAG:
            group = 2 * span
            ng = chunk // group
            g3 = gc.reshape(ng, group, HGRN_DIM)
            gref = g3[:, span:span + 1, :]
            q_up = qc.reshape(ng, group, HGRN_DIM)[:, span:, :] * jnp.exp2(g3[:, span:, :] - gref)
            k_lo = kc.reshape(ng, group, HGRN_DIM)[:, :span, :] * jnp.exp2(gref - g3[:, :span, :])
            ql = jnp.concatenate([zeros[span], q_up], axis=1).reshape(chunk, HGRN_DIM)
            kl = jnp.concatenate([k_lo, zeros[span]], axis=1).reshape(chunk, HGRN_DIM)
            p = _nt_dot(ql.astype(jnp.bfloat16), kl.astype(jnp.bfloat16))
            a = p if a is None else jnp.where((rt // group) == (cc // group), p, a)
            span //= 2
        dref = gc - _group_ref(gc, HGRN_DIAG, HGRN_DIAG // 2)
        p = _nt_dot((qc * jnp.exp2(dref)).astype(jnp.bfloat16),
                    (kc * jnp.exp2(-dref)).astype(jnp.bfloat16))
        intra.append(jnp.where(diag_mask, p, a).astype(jnp.bfloat16))

    o_intra, upds, qgs, decs = [], [], [], []
    for c, sl in enumerate(slices):
        gc = gcs[c]
        vc = v_ref[sl, :]
        g_last = gc[chunk - 1:chunk]
        k_dec = (k_in[sl] * jnp.exp2(g_last - gc)).astype(jnp.bfloat16)
        o_intra.append(jnp.dot(intra[c], vc, preferred_element_type=jnp.float32))
        upds.append(lax.dot_general(vc, k_dec, (((0,), (0,)), ((), ())),
                                    preferred_element_type=jnp.float32))
        qgs.append((q_ref[sl, :] * jnp.exp2(gc)).astype(jnp.bfloat16))
        decs.append(jnp.exp2(g_last))

    st = st_ref[...]
    for c, sl in enumerate(slices):
        o = o_intra[c] + _nt_dot(qgs[c], st.astype(jnp.bfloat16))
        st = st * decs[c] + upds[c]

        ms = jnp.mean(o * o, axis=-1, keepdims=True)
        y = o * lax.rsqrt(ms + RMS_EPS) * nw
        gate = g_ref[sl, :]
        y = y * (gate * (1.0 / (1.0 + jnp.exp(-gate))))
        o_ref[sl, :] = y.astype(o_ref.dtype)
    st_ref[...] = st


def _hgrn(hqz, hg, hv, lb_logits, norm_w, layer, batch, seq, rows=1024):
    rows = min(rows, seq)
    h = HGRN_HEADS
    d = HGRN_DIM
    depth = lb_logits.shape[0]
    head = lambda b, hh, r: (b, r, hh)
    return pl.pallas_call(
        functools.partial(_hgrn_kernel, layer=layer, rows=rows, chunk=HGRN_CHUNK),
        grid=(batch, h, seq // rows),
        in_specs=[pl.BlockSpec((None, rows, d), head),
                  pl.BlockSpec((None, rows, d), lambda b, hh, r: (b, r, h + hh)),
                  pl.BlockSpec((None, rows, d), head),
                  pl.BlockSpec((None, rows, d), head),
                  pl.BlockSpec((depth, d), lambda b, hh, r: (0, hh)),
                  pl.BlockSpec((None, 1, d), lambda b, hh, r: (layer, 0, hh))],
        out_specs=pl.BlockSpec((None, rows, d), head),
        out_shape=jax.ShapeDtypeStruct((batch, seq, h * d), jnp.bfloat16),
        scratch_shapes=[pltpu.VMEM((d, d), jnp.float32)],
        compiler_params=_params("arbitrary", "arbitrary", "arbitrary"),
        name="hgrn2",
    )(hqz, hqz, hg, hv, lb_logits, norm_w)


def _mix_out_kernel(a1_ref, a2_ref, w1_ref, w2_ref, x_ref, g_ref, b_ref, o_ref, ob_ref,
                    wb_ref, *, alpha):
    @pl.when(pl.program_id(0) == 0)
    def _():
        wb_ref[0] = w1_ref[...].astype(wb_ref.dtype)
        wb_ref[1] = w2_ref[...].astype(wb_ref.dtype)

    for sl in _row_slices(x_ref.shape[0]):
        acc = jnp.dot(a1_ref[sl, :], wb_ref[0], preferred_element_type=jnp.float32)
        acc = acc + jnp.dot(a2_ref[sl, :], wb_ref[1], preferred_element_type=jnp.float32)
        o = _layer_norm_rows(alpha * x_ref[sl, :] + acc, g_ref[...], b_ref[...])
        o_ref[sl, :] = o
        ob_ref[sl, :] = o.astype(ob_ref.dtype)


def _resident(block_shape, index_map):
    return pl.BlockSpec(block_shape, index_map, pipeline_mode=pl.Buffered(1))


def _mix_out(a1, a2, w, x, g, b, layer, alpha, tm=512):
    m, d = x.shape
    k = a1.shape[1]
    tm = min(tm, m)
    row = lambda i: (i, 0)
    par = lambda i: (layer, 0, 0)
    return pl.pallas_call(
        functools.partial(_mix_out_kernel, alpha=alpha),
        grid=(m // tm,),
        in_specs=[pl.BlockSpec((tm, k), row), pl.BlockSpec((tm, k), row),
                  _resident((None, k, d), par), _resident((None, k, d), lambda i: (layer, 1, 0)),
                  pl.BlockSpec((tm, d), row),
                  _resident((None, 1, d), par), _resident((None, 1, d), par)],
        out_specs=[pl.BlockSpec((tm, d), row), pl.BlockSpec((tm, d), row)],
        out_shape=[jax.ShapeDtypeStruct((m, d), jnp.float32),
                   jax.ShapeDtypeStruct((m, d), jnp.bfloat16)],
        scratch_shapes=[pltpu.VMEM((2, k, d), jnp.bfloat16)],
        compiler_params=_params("arbitrary"),
        name="mix_out_ln",
    )(a1, a2, w, w, x, g, b)


def _xattn_kernel(q_ref, k_ref, v_ref, wo_ref, x_ref, g_ref, b_ref, o_ref, ob_ref,
                  wb_ref, *, alpha, heads):
    @pl.when(pl.program_id(0) == 0)
    def _():
        wb_ref[...] = wo_ref[...].astype(wb_ref.dtype)

    d = q_ref.shape[1]
    dh = d // heads
    outs = []
    scores = [_nt_dot(q_ref[:, h * dh:(h + 1) * dh], k_ref[:, h * dh:(h + 1) * dh])
              for h in range(heads)]
    for h in range(heads):
        sl = slice(h * dh, (h + 1) * dh)
        s = scores[h]
        p = jnp.exp2(s - jnp.max(s, axis=-1, keepdims=True))
        p = p / jnp.sum(p, axis=-1, keepdims=True)
        outs.append(jnp.dot(p.astype(jnp.bfloat16), v_ref[:, sl],
                            preferred_element_type=jnp.float32).astype(jnp.bfloat16))
    att = jnp.concatenate(outs, axis=1)
    for sl in _row_slices(x_ref.shape[0]):
        acc = jnp.dot(att[sl], wb_ref[...], preferred_element_type=jnp.float32)
        o = _layer_norm_rows(alpha * x_ref[sl, :] + acc, g_ref[...], b_ref[...])
        o_ref[sl, :] = o
        ob_ref[sl, :] = o.astype(ob_ref.dtype)


def _xattn(q, k, v, wo, x, g, b, layer, alpha, batch, seq, tm=256):
    m, d = x.shape
    n_mem = k.shape[0] // batch
    tm = min(tm, seq)
    per_b = seq // tm
    row = lambda i: (i, 0)
    par = lambda i: (layer, 0, 0)
    mem_map = lambda i: (i // per_b, 0)
    return pl.pallas_call(
        functools.partial(_xattn_kernel, alpha=alpha, heads=XATTN_HEADS),
        grid=(m // tm,),
        in_specs=[pl.BlockSpec((tm, d), row),
                  pl.BlockSpec((n_mem, d), mem_map),
                  pl.BlockSpec((n_mem, d), mem_map),
                  _resident((None, d, d), par),
                  pl.BlockSpec((tm, d), row),
                  _resident((None, 1, d), par), _resident((None, 1, d), par)],
        out_specs=[pl.BlockSpec((tm, d), row), pl.BlockSpec((tm, d), row)],
        out_shape=[jax.ShapeDtypeStruct((m, d), jnp.float32),
                   jax.ShapeDtypeStruct((m, d), jnp.bfloat16)],
        scratch_shapes=[pltpu.VMEM((d, d), jnp.bfloat16)],
        compiler_params=_params("arbitrary"),
        name="xattn_out_ln",
    )(q, k, v, wo, x, g, b)


def _ffn_up_kernel(xb_ref, wa_ref, wb_ref, cwa_ref, cwb_ref, cba_ref, cbb_ref, o_ref,
                   wab_ref, halo_ref, *, tiles_per_seq, rs):
    i = pl.program_id(1)
    tm = xb_ref.shape[0]

    @pl.when(i == 0)
    def _():
        wab_ref[0] = wa_ref[...].astype(wab_ref.dtype)
        wab_ref[1] = wb_ref[...].astype(wab_ref.dtype)

    @pl.when(i % tiles_per_seq == 0)
    def _():
        halo_ref[...] = jnp.zeros_like(halo_ref)

    cws = (cwa_ref[...], cwb_ref[...])
    cbs = (cba_ref[...], cbb_ref[...])
    prev = [halo_ref[0], halo_ref[1]]
    for r in range(tm // rs):
        xr = xb_ref[r * rs:(r + 1) * rs, :]
        branches = []
        for t in range(2):
            h = jnp.dot(xr, wab_ref[t], preferred_element_type=jnp.float32)
            seam = jnp.concatenate([prev[t], h[:SUBLANES]], axis=0)
            h1 = jnp.concatenate([seam[SUBLANES - 1:2 * SUBLANES - 1],
                                  pltpu.roll(h, 1, axis=0)[SUBLANES:]], axis=0)
            h2 = jnp.concatenate([seam[SUBLANES - 2:2 * SUBLANES - 2],
                                  pltpu.roll(h, 2, axis=0)[SUBLANES:]], axis=0)
            prev[t] = h[rs - SUBLANES:rs]
            cw = cws[t]
            branches.append(cw[0:1] * h2 + cw[1:2] * h1 + cw[2:3] * h + cbs[t])
        a, bb = branches
        gated = (a * (1.0 / (1.0 + jnp.exp(-a)))) * bb
        o_ref[r * rs:(r + 1) * rs, :] = gated.astype(o_ref.dtype)
    halo_ref[0] = prev[0]
    halo_ref[1] = prev[1]


def _ffn_up(xb, w_up, conv_w, conv_b, layer, seq, tm=2048, tf=512, rs=256):
    m, d = xb.shape
    dff = w_up.shape[2] // 2
    tm = min(tm, seq)
    nj = dff // tf
    col_a = lambda j, i: (layer, 0, j)
    col_b = lambda j, i: (layer, 0, nj + j)
    return pl.pallas_call(
        functools.partial(_ffn_up_kernel, tiles_per_seq=seq // tm, rs=rs),
        grid=(nj, m // tm),
        in_specs=[pl.BlockSpec((tm, d), lambda j, i: (i, 0)),
                  pl.BlockSpec((None, d, tf), col_a), pl.BlockSpec((None, d, tf), col_b),
                  pl.BlockSpec((None, CONV_WIDTH, tf), col_a),
                  pl.BlockSpec((None, CONV_WIDTH, tf), col_b),
                  pl.BlockSpec((None, 1, tf), col_a), pl.BlockSpec((None, 1, tf), col_b)],
        out_specs=pl.BlockSpec((tm, tf), lambda j, i: (i, j)),
        out_shape=jax.ShapeDtypeStruct((m, dff), jnp.bfloat16),
        scratch_shapes=[pltpu.VMEM((2, d, tf), jnp.bfloat16),
                        pltpu.VMEM((2, SUBLANES, tf), jnp.float32)],
        compiler_params=_params("arbitrary", "arbitrary"),
        name="ffn_up_conv_gate",
    )(xb, w_up, w_up, conv_w, conv_w, conv_b, conv_b)


def _ffn_down_kernel(a_ref, w_ref, x_ref, g_ref, b_ref, o_ref, ob_ref, *, alpha):
    for sl in _row_slices(x_ref.shape[0]):
        acc = jnp.dot(a_ref[sl, :], w_ref[...], preferred_element_type=jnp.float32)
        o = _layer_norm_rows(alpha * x_ref[sl, :] + acc, g_ref[...], b_ref[...])
        o_ref[sl, :] = o
        ob_ref[sl, :] = o.astype(ob_ref.dtype)


def _ffn_down(a, w, x, g, b, layer, alpha, tm=256):
    m, d = x.shape
    f = a.shape[1]
    tm = min(tm, m)
    row = lambda i: (i, 0)
    par = lambda i: (layer, 0, 0)
    return pl.pallas_call(
        functools.partial(_ffn_down_kernel, alpha=alpha),
        grid=(m // tm,),
        in_specs=[pl.BlockSpec((tm, f), row),
                  _resident((f, d), lambda i: (0, 0)),
                  pl.BlockSpec((tm, d), row),
                  _resident((None, 1, d), par), _resident((None, 1, d), par)],
        out_specs=[pl.BlockSpec((tm, d), row), pl.BlockSpec((tm, d), row)],
        out_shape=[jax.ShapeDtypeStruct((m, d), jnp.float32),
                   jax.ShapeDtypeStruct((m, d), jnp.bfloat16)],
        compiler_params=_params("arbitrary"),
        name="ffn_down_ln",
    )(a, w, x, g, b)


def kernel(x, mem, w_in, fox_f_bias, hgrn_lb_logits, hgrn_norm_w, w_out, ln1_g, ln1_b,
           xq_w, xk_w, xv_w, xo_w, ln2_g, ln2_b, ffn_up, conv_w, conv_b, ffn_down,
           ln3_g, ln3_b):
    batch, seq, d = x.shape
    depth = w_in.shape[0]
    alpha = (2 * depth) ** 0.25
    bf = jnp.bfloat16
    fw = FOX_HEADS * FOX_HEAD_DIM
    hw = HGRN_HEADS * HGRN_DIM
    m = batch * seq
    n_mem = mem.shape[1]

    o_fq, o_fk, o_fv, o_ff = 0, fw, 2 * fw, 3 * fw
    o_hq = o_ff + FOX_HEADS
    o_hf, o_hi, o_hg = o_hq + hw, o_hq + 2 * hw, o_hq + 3 * hw

    tn = 1024
    ones = jnp.ones((1, tn), jnp.float32)
    qk_scale = jnp.concatenate(
        [jnp.full((1, fw), FOX_HEAD_DIM ** -0.5 * LOG2E, jnp.float32), ones], axis=1)
    hqz_scale = jnp.concatenate([jnp.full((1, hw), HGRN_DIM ** -0.5, jnp.float32), ones], axis=1)
    xq_scale = jnp.full((1, d), (d // XATTN_HEADS) ** -0.5 * LOG2E, jnp.float32)
    d_ones = jnp.ones((1, d), jnp.float32)

    xf = x.reshape(m, d)
    xb = xf.astype(bf)
    memb = mem.reshape(batch * n_mem, d).astype(bf)
    norm_w = hgrn_norm_w.reshape(depth, 1, hw)
    as_par = lambda p: p.reshape(depth, 1, -1)
    ln1 = (as_par(ln1_g), as_par(ln1_b))
    ln2 = (as_par(ln2_g), as_par(ln2_b))
    ln3 = (as_par(ln3_g), as_par(ln3_b))
    conv_b3 = as_par(conv_b)

    w_in_t = jnp.swapaxes(w_in, 1, 2)

    for l in range(depth):
        f_bias = jnp.zeros((1, FGATE_LANES), jnp.float32).at[0, :FOX_HEADS].set(fox_f_bias[l])

        qk = _matmul_nt(xb, w_in_t, l, o_fq, 2, qk_scale, bf, 1024, tn, "in_proj_qk")
        vt = _matmul_nt(xb, w_in_t, l, o_fv, 1, ones, bf, 1024, tn, "in_proj_vt",
                        transpose_out=True)
        hqz = _matmul_nt(xb, w_in_t, l, o_hq, 2, hqz_scale, jnp.float32, 1024, tn, "in_proj_hqz")
        hv = _matmul_nt(xb, w_in_t, l, o_hi, 1, ones, bf, 1024, tn, "in_proj_hv")
        hg = _matmul_nt(xb, w_in_t, l, o_hg, 1, ones, jnp.float32, 1024, tn, "in_proj_hg")
        ka, qa = _fox_bias_operands(xb, w_in_t, l, o_ff, f_bias, batch, seq)
        fox_out = _fox_attention(qk.reshape(batch, seq, 2 * fw), qa.reshape(batch, seq, fw),
                                 ka.reshape(batch, seq, fw), vt, batch, seq)
        h_out = _hgrn(hqz.reshape(batch, seq, 2 * hw), hg.reshape(batch, seq, hw),
                      hv.reshape(batch, seq, hw), hgrn_lb_logits, norm_w, l, batch, seq)
        xf, xb = _mix_out(fox_out.reshape(m, fw), h_out.reshape(m, hw), w_out, xf, *ln1, l, alpha)

        q = _matmul(xb, xq_w, l, xq_scale, bf, 1024, tn, "xattn_q")
        k = _matmul(memb, xk_w, l, d_ones, bf, 1024, tn, "xattn_k")
        v = _matmul(memb, xv_w, l, d_ones, bf, 1024, tn, "xattn_v")
        xf, xb = _xattn(q, k, v, xo_w, xf, *ln2, l, alpha, batch, seq)

        gated = _ffn_up(xb, ffn_up, conv_w, conv_b3, l, seq)
        xf, xb = _ffn_down(gated, _cast_layer(ffn_down, l), xf, *ln3, l, alpha)
    return xf.reshape(batch, seq, d)
```
